```python
import math
import jax
import jax.numpy as jnp
from jax import lax
import numpy as np

D_MODEL = 1024
BATCH = 8
SEQ = 2048
DEPTH = 4
DEC_BATCH = 128
DEC_SEQ = 8
PAST_LEN = 8192
PAGE_SIZE = 128

HEAD_DIM = 64
N_EVEN = (DEPTH + 1) // 2
N_ODD = DEPTH // 2
Q_BLOCK = 128
EPS = 1e-6
NEG = -1e30
NSA_HEADS = 8
NSA_KV_HEADS = 1
CMP_BLOCK = 32
SEL_BLOCK = 64
N_SEL = 16
WINDOW = 512
FORCE_SCORE = 1e4
FOX_HEADS = 8
FOX_KV_HEADS = 2
SB_HEADS = 8
SB_KV_HEADS = 2
MLA_HEADS = 8
Q_LORA = 256
KV_LORA = 128
NOPE_DIM = 64
ROPE_DIM = 32
V_DIM = 64
ROPE_BASE = 10000.0
MLA_SCALE = 1.0 / math.sqrt(NOPE_DIM + ROPE_DIM)

NSA_W = NSA_HEADS * HEAD_DIM
FOX_W = FOX_HEADS * HEAD_DIM
SB_W = SB_HEADS * HEAD_DIM
MLA_W = MLA_HEADS * V_DIM
MIX_W_EVEN = NSA_W + FOX_W
MIX_W_ODD = SB_W + MLA_W
E_SPLITS = (NSA_W, 6 * NSA_KV_HEADS * HEAD_DIM, 3 * NSA_HEADS, NSA_W,
            FOX_W, 2 * FOX_KV_HEADS * HEAD_DIM, FOX_HEADS, FOX_W)
O_SPLITS = (SB_W, 2 * SB_KV_HEADS * HEAD_DIM, SB_W,
            Q_LORA, KV_LORA, ROPE_DIM, MLA_W)
E_COLS = sum(E_SPLITS)
O_COLS = sum(O_SPLITS)

kernel_name = 'hybrid_nsa_fox_stickbreak_mla_decode_step'


def _split(a, sizes):
    offs = np.cumsum(sizes)[:-1].tolist()
    return jnp.split(a, offs, axis=-1)


def rmsnorm(x, g):
    xf = x.astype(jnp.float32)
    y = xf * lax.rsqrt(jnp.mean(xf * xf, axis=-1, keepdims=True) + EPS)
    return (y * g.astype(jnp.float32)).astype(x.dtype)


def alibi_slopes(n):
    return jnp.asarray([2.0 ** (-8.0 * (h + 1) / n) for h in range(n)], jnp.float32)


def rope(x, pos):
    half = ROPE_DIM // 2
    inv = ROPE_BASE ** (-jnp.arange(half, dtype=jnp.float32) / half)
    ang = pos.astype(jnp.float32)[:, None] * inv[None, :]
    cos = jnp.cos(ang)[None, :, None, :]
    sin = jnp.sin(ang)[None, :, None, :]
    xf = x.astype(jnp.float32)
    x1, x2 = xf[..., :half], xf[..., half:]
    return jnp.concatenate([x1 * cos - x2 * sin, x1 * sin + x2 * cos], axis=-1).astype(x.dtype)


def gather_pages(pool, li, page_table):
    g = pool[li, page_table]
    return g.reshape(page_table.shape[0], page_table.shape[1] * pool.shape[2], *pool.shape[3:])


def sweep_query_blocks(fn, T, *q_args):
    nb = T // Q_BLOCK
    blocks = tuple(jnp.swapaxes(a.reshape(a.shape[0], nb, Q_BLOCK, *a.shape[2:]), 0, 1) for a in q_args)
    starts = jnp.arange(nb, dtype=jnp.int32) * Q_BLOCK
    out = lax.map(lambda xs: fn(xs[0], *xs[1]), (starts, blocks))
    out = jnp.swapaxes(out, 0, 1)
    return out.reshape(out.shape[0], T, *out.shape[3:])


def compress(k, pe):
    B, Lp, G, d = k.shape
    kb = k.reshape(B, Lp // CMP_BLOCK, CMP_BLOCK, G, d)
    return jnp.mean(kb + pe[None, None, :, None, :], axis=2)


def nsa_attend(q, qpos, gates, kc, vc, ks, vs, kw, vw, wpos, slopes):
    B, Tq, H, d = q.shape
    G = kc.shape[2]
    R = H // G
    scale = 1.0 / math.sqrt(d)
    qg = q.reshape(B, Tq, G, R, d)
    m = slopes.reshape(G, R)[None, :, :, None, None]
    qf = qpos.astype(jnp.float32)
    NC = kc.shape[1]
    cidx = jnp.arange(NC, dtype=jnp.int32)
    c_end = (cidx + 1) * CMP_BLOCK - 1
    c_mid = cidx.astype(jnp.float32) * CMP_BLOCK + 0.5 * (CMP_BLOCK - 1)
    sc = jnp.einsum('bqgrd,bcgd->bgrqc', qg, kc).astype(jnp.float32) * scale - m * (qf[:, None] - c_mid[None, :])
    c_ok = c_end[None, :] <= qpos[:, None]
    pc = jax.nn.softmax(jnp.where(c_ok, sc, NEG), axis=-1) * c_ok
    oc = jnp.einsum('bgrqc,bcgd->bqgrd', pc.astype(vc.dtype), vc)
    NS = ks.shape[1] // SEL_BLOCK
    imp = pc.sum(axis=2).reshape(B, G, Tq, NS, SEL_BLOCK // CMP_BLOCK).sum(axis=-1)
    blk = jnp.arange(NS, dtype=jnp.int32)[None, :]
    cur = (qpos // SEL_BLOCK)[:, None]
    valid = blk <= cur
    forced = valid & ((blk == 0) | (blk == cur) | (blk == cur - 1))
    score = jnp.where(forced, FORCE_SCORE, jnp.where(valid, imp, -1.0))
    n_sel = min(N_SEL, NS)
    _, sel = lax.top_k(score, n_sel)
    ks_b = ks.reshape(B, NS, SEL_BLOCK, G, d).transpose(0, 3, 1, 2, 4)
    vs_b = vs.reshape(B, NS, SEL_BLOCK, G, d).transpose(0, 3, 1, 2, 4)
    bi = jnp.arange(B)[:, None, None, None]
    gi = jnp.arange(G)[None, :, None, None]
    Ksel = n_sel * SEL_BLOCK
    ks_sel = ks_b[bi, gi, sel].reshape(B, G, Tq, Ksel, d)
    vs_sel = vs_b[bi, gi, sel].reshape(B, G, Tq, Ksel, d)
    spos = (sel[..., None] * SEL_BLOCK + jnp.arange(SEL_BLOCK, dtype=jnp.int32)).reshape(B, G, Tq, Ksel)
    sdist = qpos[None, None, :, None] - spos
    ss = jnp.einsum('bqgrd,bgqkd->bgrqk', qg, ks_sel).astype(jnp.float32) * scale - m * sdist[:, :, None].astype(jnp.float32)
    s_ok = (sdist >= 0)[:, :, None]
    ps = jax.nn.softmax(jnp.where(s_ok, ss, NEG), axis=-1)
    os_ = jnp.einsum('bgrqk,bgqkd->bqgrd', ps.astype(vs.dtype), vs_sel)
    wdist = qpos[:, None] - wpos[None, :]
    w_ok = (wdist >= 0) & (wdist <= WINDOW) & (wpos[None, :] >= 0)
    sw = jnp.einsum('bqgrd,bkgd->bgrqk', qg, kw).astype(jnp.float32) * scale - m * wdist.astype(jnp.float32)
    pw = jax.nn.softmax(jnp.where(w_ok, sw, NEG), axis=-1)
    ow = jnp.einsum('bgrqk,bkgd->bqgrd', pw.astype(vw.dtype), vw)
    o = (gates[..., 0:1] * oc.reshape(B, Tq, H, d) + gates[..., 1:2] * os_.reshape(B, Tq, H, d)
         + gates[..., 2:3] * ow.reshape(B, Tq, H, d))
    return o.reshape(B, Tq, H * d)


def fox_attend(q, qcum, qpos, k, v, kcum, kpos):
    B, Tq, H, d = q.shape
    Tk, G = k.shape[1], k.shape[2]
    R = H // G
    qg = q.reshape(B, Tq, G, R, d)
    s = jnp.einsum('bqgrd,bkgd->bgrqk', qg, k).astype(jnp.float32) / math.sqrt(d)
    qc = qcum.reshape(B, Tq, G, R).transpose(0, 2, 3, 1)[..., None]
    kc = kcum.reshape(B, Tk, G, R).transpose(0, 2, 3, 1)[..., None, :]
    ok = kpos[None, :] <= qpos[:, None]
    p = jax.nn.softmax(jnp.where(ok, s + qc - kc, NEG), axis=-1)
    o = jnp.einsum('bgrqk,bkgd->bqgrd', p.astype(v.dtype), v)
    return o.reshape(B, Tq, H * d)


def sb_attend(q, qpos, k, v, kpos):
    B, Tq, H, d = q.shape
    G = k.shape[2]
    R = H // G
    qg = q.reshape(B, Tq, G, R, d)
    z = jnp.einsum('bqgrd,bkgd->bgrqk', qg, k).astype(jnp.float32) / math.sqrt(d)
    strict = kpos[None, :] < qpos[:, None]
    log_not = jnp.where(strict, -jax.nn.softplus(z), 0.0)
    after = lax.cumsum(log_not, axis=4, reverse=True) - log_not
    a = jnp.where(strict, jnp.exp(jax.nn.log_sigmoid(z) + after), 0.0)
    o = jnp.einsum('bgrqk,bkgd->bqgrd', a.astype(v.dtype), v)
    return o.reshape(B, Tq, H * d)


def mla_attend(q_lat, q_rope, qpos, ckv, krope, kpos):
    s = (jnp.einsum('bqhc,bkc->bhqk', q_lat, ckv) + jnp.einsum('bqhr,bkr->bhqk', q_rope, krope)).astype(jnp.float32) * MLA_SCALE
    ok = kpos[None, :] <= qpos[:, None]
    p = jax.nn.softmax(jnp.where(ok, s, NEG), axis=-1)
    return jnp.einsum('bhqk,bkc->bqhc', p.astype(ckv.dtype), ckv)


def even_project(xn, w_in, b_f):
    B, T, _ = xn.shape
    q_n, kv_n, g_n, z_n, q_f, kv_f, f_f, z_f = _split(xn @ w_in, E_SPLITS)
    q_n = q_n.reshape(B, T, NSA_HEADS, HEAD_DIM)
    kv_n = kv_n.reshape(B, T, 6, NSA_KV_HEADS, HEAD_DIM)
    gates = jax.nn.sigmoid(g_n.reshape(B, T, NSA_HEADS, 3))
    q_f = q_f.reshape(B, T, FOX_HEADS, HEAD_DIM)
    kv_f = kv_f.reshape(B, T, 2, FOX_KV_HEADS, HEAD_DIM)
    logf = jax.nn.log_sigmoid((f_f + b_f).astype(jnp.float32))
    return q_n, kv_n, gates, z_n, q_f, kv_f, logf, z_f


def even_merge(o_n, z_n, o_f, z_f, w_out):
    return jnp.concatenate([o_n * jax.nn.silu(z_n), o_f * jax.nn.silu(z_f)], axis=-1) @ w_out


def even_prompt(xn, w_in, b_f, nsa_pe, w_out):
    B, T, _ = xn.shape
    pos = jnp.arange(T, dtype=jnp.int32)
    q_n, kv_n, gates, z_n, q_f, kv_f, logf, z_f = even_project(xn, w_in, b_f)
    slopes = alibi_slopes(NSA_HEADS)
    kc = compress(kv_n[:, :, 0], nsa_pe[0])
    vc = compress(kv_n[:, :, 1], nsa_pe[1])
    ks, vs = kv_n[:, :, 2], kv_n[:, :, 3]
    kw_pad = jnp.pad(kv_n[:, :, 4:6], ((0, 0), (WINDOW, 0), (0, 0), (0, 0), (0, 0)))

    def nsa_block(start, qb, gb):
        qpos = start + jnp.arange(Q_BLOCK, dtype=jnp.int32)
        wb = lax.dynamic_slice_in_dim(kw_pad, start, WINDOW + Q_BLOCK, axis=1)
        wpos = start - WINDOW + jnp.arange(WINDOW + Q_BLOCK, dtype=jnp.int32)
        return nsa_attend(qb, qpos, gb, kc, vc, ks, vs, wb[:, :, 0], wb[:, :, 1], wpos, slopes)

    o_n = sweep_query_blocks(nsa_block, T, q_n, gates)
    cum = jnp.cumsum(logf, axis=1)

    def fox_block(start, qb, cb):
        qpos = start + jnp.arange(Q_BLOCK, dtype=jnp.int32)
        return fox_attend(qb, cb, qpos, kv_f[:, :, 0], kv_f[:, :, 1], cum, pos)

    o_f = sweep_query_blocks(fox_block, T, q_f, cum)
    y = even_merge(o_n, z_n, o_f, z_f, w_out)
    n_win = min(WINDOW, T)
    return y, (kv_n[:, :, :4], kv_n[:, T - n_win:, 4:6], kv_f, logf)


def even_sample(xn, nsa_pool, win_buf, fox_pool, logf_pool, li, page_table, w_in, b_f, nsa_pe, w_out):
    B, T, _ = xn.shape
    P = page_table.shape[1] * PAGE_SIZE
    L = P + T
    Lp = -(-L // SEL_BLOCK) * SEL_BLOCK
    pos = P + jnp.arange(T, dtype=jnp.int32)
    kpos = jnp.arange(L, dtype=jnp.int32)
    q_n, kv_n, gates, z_n, q_f, kv_f, logf, z_f = even_project(xn, w_in, b_f)
    slopes = alibi_slopes(NSA_HEADS)
    full = jnp.concatenate([gather_pages(nsa_pool, li, page_table), kv_n[:, :, :4]], axis=1)
    full = jnp.pad(full, ((0, 0), (0, Lp - L), (0, 0), (0, 0), (0, 0)))
    kc = compress(full[:, :, 0], nsa_pe[0])
    vc = compress(full[:, :, 1], nsa_pe[1])
    WB = win_buf.shape[1]
    win = jnp.concatenate([win_buf, kv_n[:, :, 4:6]], axis=1)
    wpos = P - WB + jnp.arange(WB + T, dtype=jnp.int32)
    o_n = nsa_attend(q_n, pos, gates, kc, vc, full[:, :, 2], full[:, :, 3], win[:, :, 0], win[:, :, 1], wpos, slopes)
    fkv = jnp.concatenate([gather_pages(fox_pool, li, page_table), kv_f], axis=1)
    past_logf = gather_pages(logf_pool, li, page_table).astype(jnp.float32)
    cum = jnp.cumsum(jnp.concatenate([past_logf, logf], axis=1), axis=1)
    o_f = fox_attend(q_f, cum[:, P:], pos, fkv[:, :, 0], fkv[:, :, 1], cum, kpos)
    y = even_merge(o_n, z_n, o_f, z_f, w_out)
    return y, (kv_n[:, :, :4], win[:, T:], kv_f, logf)


def odd_project(xn, pos, w_in, q_norm, kv_norm, w_uq, w_uk):
    B, T, _ = xn.shape
    q_s, kv_s, z_s, c_q, c_kv, k_r, z_m = _split(xn @ w_in, O_SPLITS)
    q_s = q_s.reshape(B, T, SB_HEADS, HEAD_DIM)
    kv_s = kv_s.reshape(B, T, 2, SB_KV_HEADS, HEAD_DIM)
    q_m = (rmsnorm(c_q, q_norm) @ w_uq).reshape(B, T, MLA_HEADS, NOPE_DIM + ROPE_DIM)
    q_lat = jnp.einsum('bthn,chn->bthc', q_m[..., :NOPE_DIM], w_uk)
    q_rope = rope(q_m[..., NOPE_DIM:], pos)
    latent = jnp.concatenate([rmsnorm(c_kv, kv_norm), rope(k_r[:, :, None, :], pos)[:, :, 0]], axis=-1)
    return q_s, kv_s, z_s, q_lat, q_rope, latent, z_m


def odd_merge(o_s, z_s, o_lat, z_m, w_uv, w_out):
    B, T = o_s.shape[0], o_s.shape[1]
    o_m = jnp.einsum('bqhc,chv->bqhv', o_lat, w_uv).reshape(B, T, MLA_W)
    return jnp.concatenate([o_s * jax.nn.silu(z_s), o_m * jax.nn.silu(z_m)], axis=-1) @ w_out


def odd_prompt(xn, w_in, q_norm, kv_norm, w_uq, w_uk, w_uv, w_out):
    B, T, _ = xn.shape
    pos = jnp.arange(T, dtype=jnp.int32)
    q_s, kv_s, z_s, q_lat, q_rope, latent, z_m = odd_project(xn, pos, w_in, q_norm, kv_norm, w_uq, w_uk)
    ks, vs = kv_s[:, :, 0], kv_s[:, :, 1]
    ckv, kr = latent[..., :KV_LORA], latent[..., KV_LORA:]

    def sb_block(start, qb):
        qpos = start + jnp.arange(Q_BLOCK, dtype=jnp.int32)
        return sb_attend(qb, qpos, ks, vs, pos)

    def mla_block(start, ql, qr):
        qpos = start + jnp.arange(Q_BLOCK, dtype=jnp.int32)
        return mla_attend(ql, qr, qpos, ckv, kr, pos)

    o_s = sweep_query_blocks(sb_block, T, q_s)
    o_lat = sweep_query_blocks(mla_block, T, q_lat, q_rope)
    y = odd_merge(o_s, z_s, o_lat, z_m, w_uv, w_out)
    return y, (kv_s, latent)


def odd_sample(xn, sb_pool, mla_pool, li, page_table, w_in, q_norm, kv_norm, w_uq, w_uk, w_uv, w_out):
    B, T, _ = xn.shape
    P = page_table.shape[1] * PAGE_SIZE
    pos = P + jnp.arange(T, dtype=jnp.int32)
    kpos = jnp.arange(P + T, dtype=jnp.int32)
    q_s, kv_s, z_s, q_lat, q_rope, latent, z_m = odd_project(xn, pos, w_in, q_norm, kv_norm, w_uq, w_uk)
    skv = jnp.concatenate([gather_pages(sb_pool, li, page_table), kv_s], axis=1)
    lat = jnp.concatenate([gather_pages(mla_pool, li, page_table), latent], axis=1)
    o_s = sb_attend(q_s, pos, skv[:, :, 0], skv[:, :, 1], kpos)
    o_lat = mla_attend(q_lat, q_rope, pos, lat[..., :KV_LORA], lat[..., KV_LORA:], kpos)
    y = odd_merge(o_s, z_s, o_lat, z_m, w_uv, w_out)
    return y, (kv_s, latent)


def setup_inputs(seed: int = 0) -> dict:
    key = jax.random.key(seed)
    ks = jax.random.split(key, 24)
    f32 = jnp.float32
    n_pages = PAST_LEN // PAGE_SIZE
    n_used = DEC_BATCH * n_pages
    n_pool = n_used + max(1, n_used // 4)
    win_buf = min(WINDOW, PAST_LEN)

    def nrm(k, shape, s=1.0):
        return s * jax.random.normal(k, shape, f32)

    page_table = jax.random.permutation(ks[6], n_pool)[:n_used].reshape(DEC_BATCH, n_pages).astype(jnp.int32)
    return {
        'x_prompt': nrm(ks[0], (BATCH, SEQ, D_MODEL)),
        'x_sample': nrm(ks[1], (DEC_BATCH, DEC_SEQ, D_MODEL)),
        'cache_nsa_kv': nrm(ks[2], (N_EVEN, n_pool, PAGE_SIZE, 4, NSA_KV_HEADS, HEAD_DIM)),
        'state_nsa_win': nrm(ks[3], (N_EVEN, DEC_BATCH, win_buf, 2, NSA_KV_HEADS, HEAD_DIM)),
        'cache_fox_kv': nrm(ks[4], (N_EVEN, n_pool, PAGE_SIZE, 2, FOX_KV_HEADS, HEAD_DIM)),
        'cache_fox_logf': jax.nn.log_sigmoid(2.0 + nrm(ks[5], (N_EVEN, n_pool, PAGE_SIZE, FOX_HEADS))),
        'cache_sb_kv': nrm(ks[7], (N_ODD, n_pool, PAGE_SIZE, 2, SB_KV_HEADS, HEAD_DIM)),
        'cache_mla_latent': nrm(ks[8], (N_ODD, n_pool, PAGE_SIZE, KV_LORA + ROPE_DIM)),
        'page_table': page_table,
        'norm_pre': 1.0 + nrm(ks[9], (DEPTH, D_MODEL), 0.05),
        'norm_post': 1.0 + nrm(ks[10], (DEPTH, D_MODEL), 0.05),
        'w_in_e': nrm(ks[11], (N_EVEN, D_MODEL, E_COLS), D_MODEL ** -0.5),
        'b_f': 2.0 + nrm(ks[12], (N_EVEN, FOX_HEADS), 0.5),
        'nsa_pe': nrm(ks[13], (N_EVEN, 2, CMP_BLOCK, HEAD_DIM), 0.1),
        'w_out_e': nrm(ks[14], (N_EVEN, MIX_W_EVEN, D_MODEL), MIX_W_EVEN ** -0.5),
        'w_in_o': nrm(ks[15], (N_ODD, D_MODEL, O_COLS), D_MODEL ** -0.5),
        'mla_q_norm': 1.0 + nrm(ks[16], (N_ODD, Q_LORA), 0.05),
        'mla_kv_norm': 1.0 + nrm(ks[17], (N_ODD, KV_LORA), 0.05),
        'w_uq': nrm(ks[18], (N_ODD, Q_LORA, MLA_HEADS * (NOPE_DIM + ROPE_DIM)), Q_LORA ** -0.5),
        'w_uk': nrm(ks[19], (N_ODD, KV_LORA, MLA_HEADS, NOPE_DIM), KV_LORA ** -0.5),
        'w_uv': nrm(ks[20], (N_ODD, KV_LORA, MLA_HEADS, V_DIM), KV_LORA ** -0.5),
        'w_out_o': nrm(ks[21], (N_ODD, MIX_W_ODD, D_MODEL), MIX_W_ODD ** -0.5),
    }


def reference(x_prompt, x_sample, cache_nsa_kv, state_nsa_win, cache_fox_kv, cache_fox_logf, cache_sb_kv,
              cache_mla_latent, page_table, norm_pre, norm_post, w_in_e, b_f, nsa_pe, w_out_e, w_in_o,
              mla_q_norm, mla_kv_norm, w_uq, w_uk, w_uv, w_out_o):
    xp, xs = x_prompt, x_sample
    nsa_kv_p, nsa_kv_s, win_p, win_s = [], [], [], []
    fox_kv_p, fox_kv_s, logf_p, logf_s = [], [], [], []
    sb_p, sb_s, mla_p, mla_s = [], [], [], []
    for layer in range(DEPTH):
        i = layer // 2
        hp = rmsnorm(xp, norm_pre[layer])
        hs = rmsnorm(xs, norm_pre[layer])
        if layer % 2 == 0:
            yp, (a_p, w_p, f_p, l_p) = even_prompt(hp, w_in_e[i], b_f[i], nsa_pe[i], w_out_e[i])
            ys, (a_s, w_s, f_s, l_s) = even_sample(hs, cache_nsa_kv, state_nsa_win[i], cache_fox_kv, cache_fox_logf,
                                                   i, page_table, w_in_e[i], b_f[i], nsa_pe[i], w_out_e[i])
            nsa_kv_p.append(a_p); nsa_kv_s.append(a_s)
            win_p.append(w_p); win_s.append(w_s)
            fox_kv_p.append(f_p); fox_kv_s.append(f_s)
            logf_p.append(l_p); logf_s.append(l_s)
        else:
            yp, (s_p, m_p) = odd_prompt(hp, w_in_o[i], mla_q_norm[i], mla_kv_norm[i], w_uq[i], w_uk[i], w_uv[i], w_out_o[i])
            ys, (s_s, m_s) = odd_sample(hs, cache_sb_kv, cache_mla_latent, i, page_table, w_in_o[i], mla_q_norm[i],
                                        mla_kv_norm[i], w_uq[i], w_uk[i], w_uv[i], w_out_o[i])
            sb_p.append(s_p); sb_s.append(s_s)
            mla_p.append(m_p); mla_s.append(m_s)
        xp = xp + rmsnorm(yp, norm_post[layer])
        xs = xs + rmsnorm(ys, norm_post[layer])
    return (xp, xs, jnp.stack(nsa_kv_p), jnp.stack(nsa_kv_s), jnp.stack(win_p), jnp.stack(win_s),
            jnp.stack(fox_kv_p), jnp.stack(fox_kv_s), jnp.stack(logf_p), jnp.stack(logf_s),
            jnp.stack(sb_p), jnp.stack(sb_s), jnp.stack(mla_p), jnp.stack(mla_s))
```

```python
import functools
import math

import numpy as np
import jax
import jax.numpy as jnp
from jax import lax
from jax.experimental import pallas as pl
from jax.experimental.pallas import tpu as pltpu

F32 = jnp.float32
BF16 = jnp.bfloat16
SDS = jax.ShapeDtypeStruct

D_MODEL = 1024
HEAD_DIM = 64
N_HEADS = 8
PAGE = 128
EPS = 1e-6
NEG = -1e30
CMP_BLOCK = 32
SEL_BLOCK = 64
N_SEL = 16
WINDOW = 512
FORCE_SCORE = 1e4
FOX_G = 2
SB_G = 2
Q_LORA = 256
KV_LORA = 128
NOPE_DIM = 64
ROPE_DIM = 32
V_DIM = 64
ROPE_BASE = 10000.0
MLA_SCALE = 1.0 / math.sqrt(NOPE_DIM + ROPE_DIM)
LAT = KV_LORA + ROPE_DIM
QSCALE = 1.0 / math.sqrt(HEAD_DIM)

LANES = 128
TQ = 128
TK = 128
VMEM_LIMIT = 56 * 1024 * 1024


def _params(n_axes, vmem=VMEM_LIMIT):
    return pltpu.CompilerParams(dimension_semantics=("arbitrary",) * n_axes, vmem_limit_bytes=vmem)


def _dot(a, b):
    return jnp.dot(a, b, preferred_element_type=F32)


def _dot_nt(a, b):
    return lax.dot_general(a, b, (((1,), (1,)), ((), ())), preferred_element_type=F32)


def _dot_split(x, w):
    hi = x.astype(BF16)
    lo = (x - hi.astype(F32)).astype(BF16)
    return _dot(hi, w) + _dot(lo, w)


def _rms(x, axis):
    return x * lax.rsqrt(jnp.mean(x * x, axis=axis, keepdims=True) + EPS)


def _sigmoid(x):
    return 1.0 / (1.0 + jnp.exp(-x))


def _softplus(x):
    return jnp.maximum(x, 0.0) + jnp.log1p(jnp.exp(-jnp.abs(x)))


def _silu(x):
    return x * _sigmoid(x)


def _iota(shape, axis):
    return lax.broadcasted_iota(jnp.int32, shape, axis)


def _lane_cumsum(x):
    lane = _iota(x.shape, 1)
    sh = 1
    while sh < x.shape[1]:
        x = x + jnp.where(lane >= sh, pltpu.roll(x, sh, axis=1), 0.0)
        sh *= 2
    return x


def _osm_step(s, m, l, acc, vt):
    m2 = jnp.maximum(m, jnp.max(s, axis=-1, keepdims=True))
    p = jnp.exp(s - m2)
    a = jnp.exp(m - m2)
    l2 = a * l + jnp.sum(p, axis=-1, keepdims=True)
    acc2 = a * acc + _dot_nt(p.astype(BF16), vt)
    return m2, l2, acc2


def _rep_rows(x, n):
    return jnp.concatenate([jnp.broadcast_to(x[r:r + 1], (n, x.shape[1])) for r in range(x.shape[0])], axis=0)


def _tile_rows(x, n):
    return jnp.concatenate([x] * n, axis=0)


def _head_slopes_col(rows, per_head):
    h = _iota((rows, 1), 0) // per_head
    m = jnp.zeros((rows, 1), F32)
    for hh in range(N_HEADS):
        m = jnp.where(h == hh, 2.0 ** (-(hh + 1)), m)
    return m


def _proj_even_body(x_ref, g_ref, wn_ref, wt_ref, bs_ref,
                    qn_ref, zn_ref, qf_ref, zf_ref, sm_ref, kvt_ref, kvtb_ref):
    xb = (_rms(x_ref[...], -1) * g_ref[...]).astype(BF16)
    qn_ref[...] = (_dot(xb, wn_ref[:, 0:512]) * QSCALE).astype(BF16)
    zn_ref[...] = _dot(xb, wn_ref[:, 512:1024])
    qf_ref[...] = (_dot(xb, wn_ref[:, 1024:1536]) * QSCALE).astype(BF16)
    zf_ref[...] = _dot(xb, wn_ref[:, 1536:2048])
    s = _dot(xb, wn_ref[:, 2048:2176]) + bs_ref[...]
    lane = _iota(s.shape, 1)
    sm_ref[...] = jnp.where(lane < 3 * N_HEADS, _sigmoid(s), -_softplus(-s))
    ht = _dot_nt(wt_ref[...], xb)
    kvt_ref[0] = ht
    kvtb_ref[0] = ht.astype(BF16)


def _proj_even(x2, g, wn, wt, bs, nb, tm=256):
    n = x2.shape[0]
    tm = min(tm, n)
    t = n // nb
    nt = t // tm
    row = lambda i: (i, 0)
    const = lambda i: (0, 0)
    tr = lambda i: (i // nt, 0, i % nt)
    ct = wt.shape[0]
    return pl.pallas_call(
        _proj_even_body,
        grid=(n // tm,),
        in_specs=[pl.BlockSpec((tm, D_MODEL), row), pl.BlockSpec((1, D_MODEL), const),
                  pl.BlockSpec(wn.shape, const), pl.BlockSpec(wt.shape, const), pl.BlockSpec((1, LANES), const)],
        out_specs=[pl.BlockSpec((tm, 512), row), pl.BlockSpec((tm, 512), row), pl.BlockSpec((tm, 512), row),
                   pl.BlockSpec((tm, 512), row), pl.BlockSpec((tm, LANES), row),
                   pl.BlockSpec((1, ct, tm), tr), pl.BlockSpec((1, ct, tm), tr)],
        out_shape=[SDS((n, 512), BF16), SDS((n, 512), F32), SDS((n, 512), BF16), SDS((n, 512), F32),
                   SDS((n, LANES), F32), SDS((nb, ct, t), F32), SDS((nb, ct, t), BF16)],
        compiler_params=_params(1),
        name="proj_even",
    )(x2, g, wn, wt, bs)


def _proj_odd_body(x_ref, g_ref, wn_ref, wt_ref, qn_ref, kvn_ref, wuqn_ref, wuqr_ref, wuk_ref,
                   cq_ref, sq_ref, ct_ref, st_ref,
                   qs_ref, zs_ref, zm_ref, ql_ref, qr_ref, kvt_ref, kvtb_ref, lt_ref, ltb_ref):
    xb = (_rms(x_ref[...], -1) * g_ref[...]).astype(BF16)
    qs_ref[...] = (_dot(xb, wn_ref[:, 0:512]) * QSCALE).astype(BF16)
    zs_ref[...] = _dot(xb, wn_ref[:, 512:1024])
    zm_ref[...] = _dot(xb, wn_ref[:, 1024:1536])
    cq = _dot(xb, wn_ref[:, 1536:1792])
    cb = (_rms(cq, -1) * qn_ref[...]).astype(BF16)
    nope = _dot(cb, wuqn_ref[...]).astype(BF16)
    ql_ref[...] = _dot(nope, wuk_ref[...]).astype(BF16)
    rr = _dot(cb, wuqr_ref[...])
    x1, x2 = rr[:, 0:LANES], rr[:, LANES:2 * LANES]
    cq_t, sq_t = cq_ref[...], sq_ref[...]
    qr_ref[:, 0:LANES] = (x1 * cq_t - x2 * sq_t).astype(BF16)
    qr_ref[:, LANES:2 * LANES] = (x1 * sq_t + x2 * cq_t).astype(BF16)
    ht = _dot_nt(wt_ref[...], xb)
    kvt_ref[0] = ht[0:256]
    kvtb_ref[0] = ht[0:256].astype(BF16)
    ckv = _rms(ht[256:384], 0) * kvn_ref[...]
    half = ROPE_DIM // 2
    k1, k2 = ht[384:384 + half], ht[384 + half:384 + ROPE_DIM]
    c_t, s_t = ct_ref[...], st_ref[...]
    lat = jnp.concatenate([ckv, k1 * c_t - k2 * s_t, k1 * s_t + k2 * c_t], axis=0)
    lt_ref[0] = lat
    ltb_ref[0] = lat.astype(BF16)


def _proj_odd(x2, g, wn, wt, qn, kvn, wuqn, wuqr, wukbd, cosq, sinq, cost, sint, nb, tm=256):
    n = x2.shape[0]
    tm = min(tm, n)
    t = n // nb
    nt = t // tm
    row = lambda i: (i, 0)
    const = lambda i: (0, 0)
    tr = lambda i: (i // nt, 0, i % nt)
    prow = lambda i: (i % nt, 0)
    pcol = lambda i: (0, i % nt)
    half = ROPE_DIM // 2
    return pl.pallas_call(
        _proj_odd_body,
        grid=(n // tm,),
        in_specs=[pl.BlockSpec((tm, D_MODEL), row), pl.BlockSpec((1, D_MODEL), const),
                  pl.BlockSpec(wn.shape, const), pl.BlockSpec(wt.shape, const),
                  pl.BlockSpec((1, Q_LORA), const), pl.BlockSpec((KV_LORA, 1), const),
                  pl.BlockSpec(wuqn.shape, const), pl.BlockSpec(wuqr.shape, const), pl.BlockSpec(wukbd.shape, const),
                  pl.BlockSpec((tm, LANES), prow), pl.BlockSpec((tm, LANES), prow),
                  pl.BlockSpec((half, tm), pcol), pl.BlockSpec((half, tm), pcol)],
        out_specs=[pl.BlockSpec((tm, 512), row), pl.BlockSpec((tm, 512), row), pl.BlockSpec((tm, 512), row),
                   pl.BlockSpec((tm, 1024), row), pl.BlockSpec((tm, 256), row),
                   pl.BlockSpec((1, 256, tm), tr), pl.BlockSpec((1, 256, tm), tr),
                   pl.BlockSpec((1, LAT, tm), tr), pl.BlockSpec((1, LAT, tm), tr)],
        out_shape=[SDS((n, 512), BF16), SDS((n, 512), F32), SDS((n, 512), F32),
                   SDS((n, 1024), BF16), SDS((n, 256), BF16),
                   SDS((nb, 256, t), F32), SDS((nb, 256, t), BF16),
                   SDS((nb, LAT, t), F32), SDS((nb, LAT, t), BF16)],
        compiler_params=_params(1),
        name="proj_odd",
    )(x2, g, wn, wt, qn, kvn, wuqn, wuqr, wukbd, cosq, sinq, cost, sint)


def _merge_body(odd, oa_ref, za_ref, ob_ref, zb_ref, wo_ref, wuv_ref, gp_ref, x_ref, out_ref):
    a = (oa_ref[...] * _silu(za_ref[...])).astype(BF16)
    if odd:
        ob = _dot(ob_ref[...].astype(BF16), wuv_ref[...])
    else:
        ob = ob_ref[...]
    b = (ob * _silu(zb_ref[...])).astype(BF16)
    y = _dot(a, wo_ref[0:512, :]) + _dot(b, wo_ref[512:1024, :])
    out_ref[...] = x_ref[...] + _rms(y, -1) * gp_ref[...]


def _merge(odd, oa, za, ob, zb, wo, wuv, gp, x2, tm=256):
    n = x2.shape[0]
    tm = min(tm, n)
    row = lambda i: (i, 0)
    const = lambda i: (0, 0)
    return pl.pallas_call(
        functools.partial(_merge_body, odd),
        grid=(n // tm,),
        in_specs=[pl.BlockSpec((tm, 512), row), pl.BlockSpec((tm, 512), row),
                  pl.BlockSpec((tm, ob.shape[1]), row), pl.BlockSpec((tm, 512), row),
                  pl.BlockSpec(wo.shape, const), pl.BlockSpec(wuv.shape, const),
                  pl.BlockSpec((1, D_MODEL), const), pl.BlockSpec((tm, D_MODEL), row)],
        out_specs=pl.BlockSpec((tm, D_MODEL), row),
        out_shape=SDS((n, D_MODEL), F32),
        compiler_params=_params(1),
        name="merge_odd" if odd else "merge_even",
    )(oa, za, ob, zb, wo, wuv, gp, x2)


def _cumsum_body(lf_ref, out_ref):
    t = lf_ref.shape[2]
    carry = jnp.zeros((lf_ref.shape[1], 1), F32)
    for c in range(t // LANES):
        cs = _lane_cumsum(lf_ref[0, :, c * LANES:(c + 1) * LANES]) + carry
        out_ref[0, :, c * LANES:(c + 1) * LANES] = cs
        carry = cs[:, LANES - 1:LANES]


def _cumsum_lanes(lft):
    b, h, t = lft.shape
    spec = pl.BlockSpec((1, h, t), lambda i: (i, 0, 0))
    return pl.pallas_call(_cumsum_body, grid=(b,), in_specs=[spec], out_specs=spec,
                          out_shape=SDS(lft.shape, F32), compiler_params=_params(1), name="fox_cumsum")(lft)


def _fox_prompt_body(q_ref, kt_ref, vt_ref, cq_ref, ck_ref, o_ref):
    i = pl.program_id(2)
    r_heads = q_ref.shape[1]
    qpos = i * TQ + _iota((TQ, 1), 0)
    lane = _iota((1, TK), 1)
    for r in range(r_heads):
        q = q_ref[0, r]
        cq = cq_ref[0, 0][:, r:r + 1]

        def step(j, carry, q=q, cq=cq, r=r):
            m, l, acc = carry
            s = _dot(q, kt_ref[0, 0, j]) + cq - ck_ref[0, 0, j][r:r + 1, :]
            s = jnp.where(j * TK + lane <= qpos, s, NEG)
            return _osm_step(s, m, l, acc, vt_ref[0, 0, j])

        init = (jnp.full((TQ, 1), NEG, F32), jnp.zeros((TQ, 1), F32), jnp.zeros((TQ, HEAD_DIM), F32))
        m, l, acc = lax.fori_loop(0, i + 1, step, init)
        o_ref[0, r] = acc / l


def _fox_prompt(q, ktb, vtb, cumq, cumk):
    b, h, t, d = q.shape
    g = ktb.shape[1]
    r = h // g
    nk = t // TK
    return pl.pallas_call(
        _fox_prompt_body,
        grid=(b, g, t // TQ),
        in_specs=[pl.BlockSpec((1, r, TQ, d), lambda bi, gi, i: (bi, gi, i, 0)),
                  pl.BlockSpec((1, 1, nk, d, TK), lambda bi, gi, i: (bi, gi, 0, 0, 0)),
                  pl.BlockSpec((1, 1, nk, d, TK), lambda bi, gi, i: (bi, gi, 0, 0, 0)),
                  pl.BlockSpec((1, 1, TQ, r), lambda bi, gi, i: (bi, gi, i, 0)),
                  pl.BlockSpec((1, 1, nk, r, TK), lambda bi, gi, i: (bi, gi, 0, 0, 0))],
        out_specs=pl.BlockSpec((1, r, TQ, d), lambda bi, gi, i: (bi, gi, i, 0)),
        out_shape=SDS((b, h, t, d), F32),
        compiler_params=_params(3),
        name="fox_prompt",
    )(q, ktb, vtb, cumq, cumk)


def _sb_block(z, strict, suf, acc, vt, u):
    sp = _softplus(z)
    ln = jnp.where(strict, -sp, 0.0)
    aft = _dot_split(ln, u) + suf
    a = jnp.where(strict, jnp.exp(z - sp + aft), 0.0)
    acc = acc + _dot_nt(a.astype(BF16), vt)
    return suf + jnp.sum(ln, axis=-1, keepdims=True), acc


def _sb_prompt_body(q_ref, kt_ref, vt_ref, u_ref, o_ref):
    i = pl.program_id(2)
    r_heads = q_ref.shape[1]
    qpos = i * TQ + _iota((TQ, 1), 0)
    lane = _iota((1, TK), 1)
    u = u_ref[...]
    for r in range(r_heads):
        q = q_ref[0, r]

        def step(jj, carry, q=q):
            suf, acc = carry
            j = i - jj
            z = _dot(q, kt_ref[0, 0, j])
            return _sb_block(z, j * TK + lane < qpos, suf, acc, vt_ref[0, 0, j], u)

        init = (jnp.zeros((TQ, 1), F32), jnp.zeros((TQ, HEAD_DIM), F32))
        _, acc = lax.fori_loop(0, i + 1, step, init)
        o_ref[0, r] = acc


def _suffix_matrix():
    a = np.arange(TK)
    return jnp.asarray((a[:, None] > a[None, :]).astype(np.float32), BF16)


def _sb_prompt(q, ktb, vtb):
    b, h, t, d = q.shape
    g = ktb.shape[1]
    r = h // g
    nk = t // TK
    return pl.pallas_call(
        _sb_prompt_body,
        grid=(b, g, t // TQ),
        in_specs=[pl.BlockSpec((1, r, TQ, d), lambda bi, gi, i: (bi, gi, i, 0)),
                  pl.BlockSpec((1, 1, nk, d, TK), lambda bi, gi, i: (bi, gi, 0, 0, 0)),
                  pl.BlockSpec((1, 1, nk, d, TK), lambda bi, gi, i: (bi, gi, 0, 0, 0)),
                  pl.BlockSpec((TK, TK), lambda bi, gi, i: (0, 0))],
        out_specs=pl.BlockSpec((1, r, TQ, d), lambda bi, gi, i: (bi, gi, i, 0)),
        out_shape=SDS((b, h, t, d), F32),
        compiler_params=_params(3),
        name="sb_prompt",
    )(q, ktb, vtb, _suffix_matrix())


def _mla_prompt_body(ql_ref, qr_ref, ck_ref, kr_ref, o_ref):
    i = pl.program_id(2)
    qpos = i * TQ + _iota((TQ, 1), 0)
    lane = _iota((1, TK), 1)
    ql = ql_ref[0, 0]
    qr = qr_ref[0, 0]

    def step(j, carry):
        m, l, acc = carry
        ck = ck_ref[0, j]
        s = (_dot(ql, ck) + _dot(qr, kr_ref[0, j])) * MLA_SCALE
        s = jnp.where(j * TK + lane <= qpos, s, NEG)
        return _osm_step(s, m, l, acc, ck)

    init = (jnp.full((TQ, 1), NEG, F32), jnp.zeros((TQ, 1), F32), jnp.zeros((TQ, KV_LORA), F32))
    m, l, acc = lax.fori_loop(0, i + 1, step, init)
    o_ref[0, 0] = acc / l


def _mla_prompt(ql, qr, ckb, krb):
    b, h, t, c = ql.shape
    nk = t // TK
    return pl.pallas_call(
        _mla_prompt_body,
        grid=(b, h, t // TQ),
        in_specs=[pl.BlockSpec((1, 1, TQ, c), lambda bi, hi, i: (bi, hi, i, 0)),
                  pl.BlockSpec((1, 1, TQ, ROPE_DIM), lambda bi, hi, i: (bi, hi, i, 0)),
                  pl.BlockSpec((1, nk, c, TK), lambda bi, hi, i: (bi, 0, 0, 0)),
                  pl.BlockSpec((1, nk, ROPE_DIM, TK), lambda bi, hi, i: (bi, 0, 0, 0))],
        out_specs=pl.BlockSpec((1, 1, TQ, c), lambda bi, hi, i: (bi, hi, i, 0)),
        out_shape=SDS((b, h, t, c), F32),
        compiler_params=_params(3),
        name="mla_prompt",
    )(ql, qr, ckb, krb)


def _nsa_compressed(q, kc, vct, mcol, qpos, n_real):
    ncp = kc.shape[1]
    lane = _iota((1, ncp), 1)
    c_end = (lane + 1) * CMP_BLOCK - 1
    c_mid = lane.astype(F32) * CMP_BLOCK + 0.5 * (CMP_BLOCK - 1)
    c_ok = (c_end <= qpos) & (lane < n_real)
    sc = _dot(q, kc) - mcol * (qpos.astype(F32) - c_mid)
    sc = jnp.where(c_ok, sc, NEG)
    e = jnp.exp(sc - jnp.max(sc, axis=-1, keepdims=True))
    pc = e / jnp.sum(e, axis=-1, keepdims=True) * c_ok.astype(F32)
    return pc, _dot_nt(pc.astype(BF16), vct)


def _nsa_select(imp2, qpos, ns):
    w = imp2.shape[1]
    lane = _iota((1, w), 1)
    blk = lane // 2
    is_blk = (lane % 2 == 0) & (blk < ns)
    cur = qpos // SEL_BLOCK
    valid = is_blk & (blk <= cur)
    forced = valid & ((blk == 0) | (blk == cur) | (blk == cur - 1))
    score = jnp.where(forced, FORCE_SCORE, jnp.where(valid, imp2, -1.0))
    score = jnp.where(is_blk, score, -jnp.inf)
    cnt = jnp.zeros(score.shape, F32)
    for k in range(ns):
        sk = score[:, 2 * k:2 * k + 1]
        ahead = (sk > score) | ((sk == score) & (2 * k < lane))
        cnt = cnt + ahead.astype(F32)
    return jnp.where(is_blk & (cnt < min(N_SEL, ns)), 1.0, 0.0)


def _sel_expand_matrix(w, first_key, n_keys):
    l = _iota((w, n_keys), 0)
    k = _iota((w, n_keys), 1) + first_key
    return jnp.where(l == 2 * (k // SEL_BLOCK), 1.0, 0.0).astype(BF16)


def _nsa_compress_body(cmp_ref, pe_ref, pool_ref, out_ref, outb_ref):
    pe_mean = jnp.mean(pe_ref[...], axis=1, keepdims=True)
    kvc = _dot_split(cmp_ref[0], pool_ref[...]) + pe_mean
    out_ref[0] = kvc
    outb_ref[0] = kvc.astype(BF16)


def _nsa_compress(kvt, pet, ncp):
    b, _, t = kvt.shape
    a = np.arange(t)[:, None] // CMP_BLOCK == np.arange(ncp)[None, :]
    pool = jnp.asarray(a.astype(np.float32) / CMP_BLOCK, BF16)
    return pl.pallas_call(
        _nsa_compress_body,
        grid=(b,),
        in_specs=[pl.BlockSpec((1, 2 * HEAD_DIM, t), lambda i: (i, 0, 0)),
                  pl.BlockSpec(pet.shape, lambda i: (0, 0)), pl.BlockSpec(pool.shape, lambda i: (0, 0))],
        out_specs=[pl.BlockSpec((1, 2 * HEAD_DIM, ncp), lambda i: (i, 0, 0)),
                   pl.BlockSpec((1, 2 * HEAD_DIM, ncp), lambda i: (i, 0, 0))],
        out_shape=[SDS((b, 2 * HEAD_DIM, ncp), F32), SDS((b, 2 * HEAD_DIM, ncp), BF16)],
        compiler_params=_params(1),
        name="nsa_compress",
    )(kvt, pet, pool)


def _nsa_prompt_body(nc, ns, q_ref, sm_ref, kvc_ref, sel_ref, win_ref, o_ref, mask_ref):
    i = pl.program_id(1)
    nk = sel_ref.shape[1]
    d = HEAD_DIM
    qpos = i * TQ + _iota((TQ, 1), 0)
    qf = qpos.astype(F32)
    lane = _iota((1, TK), 1)
    kc = kvc_ref[0, 0:d, :]
    vct = kvc_ref[0, d:2 * d, :]
    ncp = kc.shape[1]
    psum = jnp.zeros((TQ, ncp), F32)
    for h in range(N_HEADS):
        pc, oc = _nsa_compressed(q_ref[0, h], kc, vct, 2.0 ** (-(h + 1)), qpos, nc)
        psum = psum + pc
        o_ref[0, h] = sm_ref[:, 3 * h:3 * h + 1] * oc
    imp2 = psum + pltpu.roll(psum, ncp - 1, axis=1)
    sel = _nsa_select(imp2, qpos, ns).astype(BF16)
    for j in range(nk):
        mask_ref[j] = _dot(sel, _sel_expand_matrix(ncp, j * TK, TK))
    lo = jnp.maximum(i - WINDOW // TK, 0)
    for h in range(N_HEADS):
        q = q_ref[0, h]
        slope = 2.0 ** (-(h + 1))

        def sel_step(j, carry, q=q, slope=slope):
            m, l, acc = carry
            kpos = j * TK + lane
            s = _dot(q, sel_ref[0, j, 0:d, :]) - slope * (qf - kpos.astype(F32))
            s = jnp.where((mask_ref[j] > 0.5) & (kpos <= qpos), s, NEG)
            return _osm_step(s, m, l, acc, sel_ref[0, j, d:2 * d, :])

        def win_step(j, carry, q=q, slope=slope):
            m, l, acc = carry
            dist = qpos - (j * TK + lane)
            s = _dot(q, win_ref[0, j, 0:d, :]) - slope * dist.astype(F32)
            s = jnp.where((dist >= 0) & (dist <= WINDOW), s, NEG)
            return _osm_step(s, m, l, acc, win_ref[0, j, d:2 * d, :])

        init = (jnp.full((TQ, 1), NEG, F32), jnp.zeros((TQ, 1), F32), jnp.zeros((TQ, d), F32))
        _, ls, accs = lax.fori_loop(0, i + 1, sel_step, init)
        _, lw, accw = lax.fori_loop(lo, i + 1, win_step, init)
        g1 = sm_ref[:, 3 * h + 1:3 * h + 2]
        g2 = sm_ref[:, 3 * h + 2:3 * h + 3]
        o_ref[0, h] = o_ref[0, h] + g1 * (accs / ls) + g2 * (accw / lw)


def _nsa_prompt(q, small, kvcb, selb, winb, nc, ns):
    b, h, t, d = q.shape
    nk = t // TK
    nq = t // TQ
    ncp = kvcb.shape[2]
    return pl.pallas_call(
        functools.partial(_nsa_prompt_body, nc, ns),
        grid=(b, nq),
        in_specs=[pl.BlockSpec((1, h, TQ, d), lambda bi, i: (bi, 0, i, 0)),
                  pl.BlockSpec((TQ, LANES), lambda bi, i: (bi * nq + i, 0)),
                  pl.BlockSpec((1, 2 * d, ncp), lambda bi, i: (bi, 0, 0)),
                  pl.BlockSpec((1, nk, 2 * d, TK), lambda bi, i: (bi, 0, 0, 0)),
                  pl.BlockSpec((1, nk, 2 * d, TK), lambda bi, i: (bi, 0, 0, 0))],
        out_specs=pl.BlockSpec((1, h, TQ, d), lambda bi, i: (bi, 0, i, 0)),
        out_shape=SDS((b, h, t, d), F32),
        scratch_shapes=[pltpu.VMEM((nk, TQ, TK), F32)],
        compiler_params=_params(2),
        name="nsa_prompt",
    )(q, small, kvcb, selb, winb)


def _fetch(pool, li, pt_ref, buf, sem, seq, slot, start):
    n_pages = buf.shape[1]

    def body(p, c):
        page = pt_ref[seq, p] if start else 0
        cp = pltpu.make_async_copy(pool.at[li, page], buf.at[slot, p], sem.at[slot])
        if start:
            cp.start()
        else:
            cp.wait()
        return c

    lax.fori_loop(0, n_pages, body, 0)


def _pipeline_pages(pools, li, pt_ref, bufs, sems):
    b = pl.program_id(0)
    nb = pl.num_programs(0)
    slot = b % 2

    @pl.when(b == 0)
    def _():
        for pool, buf, sem in zip(pools, bufs, sems):
            _fetch(pool, li, pt_ref, buf, sem, 0, 0, True)

    @pl.when(b + 1 < nb)
    def _():
        for pool, buf, sem in zip(pools, bufs, sems):
            _fetch(pool, li, pt_ref, buf, sem, b + 1, 1 - slot, True)

    for pool, buf, sem in zip(pools, bufs, sems):
        _fetch(pool, li, pt_ref, buf, sem, b, slot, False)
    return slot


def _fox_decode_body(li, past, pt_ref, q_ref, new_ref, lfn_ref, lq_ref, kv_hbm, lf_hbm, o_ref,
                     kvbuf, lfbuf, cum_ref, sem_kv, sem_lf):
    slot = _pipeline_pages((kv_hbm, lf_hbm), li, pt_ref, (kvbuf, lfbuf), (sem_kv, sem_lf))
    n_pages = kvbuf.shape[1]
    d = HEAD_DIM
    rows = q_ref.shape[1] // FOX_G
    n_tok = rows // (N_HEADS // FOX_G)
    r_heads = N_HEADS // FOX_G

    def cs_step(p, carry):
        c = _lane_cumsum(lfbuf[slot, p]) + carry
        cum_ref[p] = c
        return c[:, LANES - 1:LANES]

    total = lax.fori_loop(0, n_pages, cs_step, jnp.zeros((N_HEADS, 1), F32))
    cum_new = _lane_cumsum(lfn_ref[0])
    lane = _iota((1, LANES), 1)
    tok = _iota((rows, 1), 0) % n_tok
    for g in range(FOX_G):
        q = q_ref[0, g * rows:(g + 1) * rows, :]
        lq = lq_ref[0, g * rows:(g + 1) * rows, :]
        cq = jnp.sum(jnp.where(lane <= tok, lq, 0.0), axis=-1, keepdims=True)
        tot_g = total[g * r_heads:(g + 1) * r_heads]

        def step(p, carry, q=q, cq=cq, tot_g=tot_g, g=g):
            m, l, acc = carry
            kt = kvbuf[slot, p, g * d:(g + 1) * d, :].astype(BF16)
            vt = kvbuf[slot, p, (FOX_G + g) * d:(FOX_G + g + 1) * d, :].astype(BF16)
            rel = _rep_rows(cum_ref[p][g * r_heads:(g + 1) * r_heads, :] - tot_g, n_tok)
            s = _dot(q, kt) + cq - rel
            return _osm_step(s, m, l, acc, vt)

        init = (jnp.full((rows, 1), NEG, F32), jnp.zeros((rows, 1), F32), jnp.zeros((rows, d), F32))
        m, l, acc = lax.fori_loop(0, n_pages, step, init)
        kt = new_ref[0, g * d:(g + 1) * d, :].astype(BF16)
        vt = new_ref[0, (FOX_G + g) * d:(FOX_G + g + 1) * d, :].astype(BF16)
        rel = _rep_rows(cum_new[g * r_heads:(g + 1) * r_heads, :], n_tok)
        s = jnp.where(lane <= tok, _dot(q, kt) + cq - rel, NEG)
        m, l, acc = _osm_step(s, m, l, acc, vt)
        o_ref[0, g * rows:(g + 1) * rows, :] = acc / l


def _decode_call(body, name, pt, ins, in_specs, pools, out_block, out_shape, scratch):
    nb = pt.shape[0]
    any_spec = pl.BlockSpec(memory_space=pl.ANY)
    grid_spec = pltpu.PrefetchScalarGridSpec(
        num_scalar_prefetch=1, grid=(nb,),
        in_specs=list(in_specs) + [any_spec] * len(pools),
        out_specs=pl.BlockSpec(out_block, lambda b, pt_ref: (b, 0, 0)),
        scratch_shapes=scratch)
    return pl.pallas_call(body, grid_spec=grid_spec, out_shape=out_shape,
                          compiler_params=_params(1), name=name)(pt, *ins, *pools)


def _seq_spec(shape):
    return pl.BlockSpec((1,) + tuple(shape[1:]), lambda b, pt_ref: (b,) + (0,) * (len(shape) - 1))


def _fox_decode(li, pt, q, newt, lfnew, lq, kv_pool, lf_pool):
    s, rows, d = q.shape
    n_pages = pt.shape[1]
    past = n_pages * PAGE
    scratch = [pltpu.VMEM((2, n_pages, 4 * d, PAGE), F32), pltpu.VMEM((2, n_pages, N_HEADS, PAGE), F32),
               pltpu.VMEM((n_pages, N_HEADS, PAGE), F32),
               pltpu.SemaphoreType.DMA((2,)), pltpu.SemaphoreType.DMA((2,))]
    ins = (q, newt, lfnew, lq)
    return _decode_call(functools.partial(_fox_decode_body, li, past), "fox_decode", pt, ins,
                        [_seq_spec(a.shape) for a in ins], (kv_pool, lf_pool),
                        (1, rows, d), SDS((s, rows, d), F32), scratch)


def _sb_decode_body(li, past, pt_ref, q_ref, new_ref, u_ref, kv_hbm, o_ref, kvbuf, sem_kv):
    slot = _pipeline_pages((kv_hbm,), li, pt_ref, (kvbuf,), (sem_kv,))
    n_pages = kvbuf.shape[1]
    d = HEAD_DIM
    rows = q_ref.shape[1] // SB_G
    n_tok = rows // (N_HEADS // SB_G)
    lane = _iota((1, LANES), 1)
    tok = _iota((rows, 1), 0) % n_tok
    u = u_ref[...]
    always = jnp.full((rows, LANES), True)
    for g in range(SB_G):
        q = q_ref[0, g * rows:(g + 1) * rows, :]
        kt = new_ref[0, g * d:(g + 1) * d, :].astype(BF16)
        vt = new_ref[0, (SB_G + g) * d:(SB_G + g + 1) * d, :].astype(BF16)
        init = (jnp.zeros((rows, 1), F32), jnp.zeros((rows, d), F32))
        carry = _sb_block(_dot(q, kt), lane < tok, init[0], init[1], vt, u)

        def step(pp, carry, q=q, g=g):
            suf, acc = carry
            p = n_pages - 1 - pp
            kt = kvbuf[slot, p, g * d:(g + 1) * d, :].astype(BF16)
            vt = kvbuf[slot, p, (SB_G + g) * d:(SB_G + g + 1) * d, :].astype(BF16)
            return _sb_block(_dot(q, kt), always, suf, acc, vt, u)

        _, acc = lax.fori_loop(0, n_pages, step, carry)
        o_ref[0, g * rows:(g + 1) * rows, :] = acc


def _sb_decode(li, pt, q, newt, kv_pool):
    s, rows, d = q.shape
    n_pages = pt.shape[1]
    scratch = [pltpu.VMEM((2, n_pages, 4 * d, PAGE), F32), pltpu.SemaphoreType.DMA((2,))]
    u = _suffix_matrix()
    in_specs = [_seq_spec(q.shape), _seq_spec(newt.shape), pl.BlockSpec((TK, TK), lambda b, pt_ref: (0, 0))]
    return _decode_call(functools.partial(_sb_decode_body, li, n_pages * PAGE), "sb_decode", pt, (q, newt, u),
                        in_specs, (kv_pool,), (1, rows, d), SDS((s, rows, d), F32), scratch)


def _mla_decode_body(li, past, pt_ref, ql_ref, qr_ref, new_ref, lat_hbm, o_ref, buf, sem):
    slot = _pipeline_pages((lat_hbm,), li, pt_ref, (buf,), (sem,))
    n_pages = buf.shape[1]
    rows = ql_ref.shape[1]
    n_tok = rows // N_HEADS
    lane = _iota((1, LANES), 1)
    tok = _iota((rows, 1), 0) % n_tok
    ql = ql_ref[0]
    qr = qr_ref[0]

    def step(p, carry):
        m, l, acc = carry
        ck = buf[slot, p, 0:KV_LORA, :].astype(BF16)
        kr = buf[slot, p, KV_LORA:LAT, :].astype(BF16)
        s = (_dot(ql, ck) + _dot(qr, kr)) * MLA_SCALE
        return _osm_step(s, m, l, acc, ck)

    init = (jnp.full((rows, 1), NEG, F32), jnp.zeros((rows, 1), F32), jnp.zeros((rows, KV_LORA), F32))
    m, l, acc = lax.fori_loop(0, n_pages, step, init)
    ck = new_ref[0, 0:KV_LORA, :].astype(BF16)
    kr = new_ref[0, KV_LORA:LAT, :].astype(BF16)
    s = (_dot(ql, ck) + _dot(qr, kr)) * MLA_SCALE
    m, l, acc = _osm_step(jnp.where(lane <= tok, s, NEG), m, l, acc, ck)
    o_ref[0] = acc / l


def _mla_decode(li, pt, ql, qr, newt, lat_pool):
    s, rows, c = ql.shape
    n_pages = pt.shape[1]
    scratch = [pltpu.VMEM((2, n_pages, LAT, PAGE), F32), pltpu.SemaphoreType.DMA((2,))]
    ins = (ql, qr, newt)
    return _decode_call(functools.partial(_mla_decode_body, li, n_pages * PAGE), "mla_decode", pt, ins,
                        [_seq_spec(a.shape) for a in ins], (lat_pool,), (1, rows, c), SDS((s, rows, c), F32), scratch)


def _nsa_decode_body(li, past, ncp, pt_ref, q_ref, gate_ref, new_ref, pe_ref, win_ref, kv_hbm, o_ref,
                     kvbuf, sem_kv):
    slot = _pipeline_pages((kv_hbm,), li, pt_ref, (kvbuf,), (sem_kv,))
    n_pages = kvbuf.shape[1]
    d = HEAD_DIM
    rows = q_ref.shape[1]
    n_tok = rows // N_HEADS
    per_page = PAGE // CMP_BLOCK
    nc = n_pages * per_page
    ns = (past + n_tok + SEL_BLOCK - 1) // SEL_BLOCK
    q = q_ref[0]
    lane = _iota((1, LANES), 1)
    tok = _iota((rows, 1), 0) % n_tok
    qpos = past + tok
    qf = qpos.astype(F32)
    mcol = _head_slopes_col(rows, n_tok)

    def cmp_step(p, acc):
        key = _iota((PAGE, ncp), 0)
        col = _iota((PAGE, ncp), 1)
        pool = jnp.where(col == p * per_page + key // CMP_BLOCK, 1.0 / CMP_BLOCK, 0.0).astype(BF16)
        return acc + _dot_split(kvbuf[slot, p, 0:2 * d, :], pool)

    kvc = lax.fori_loop(0, n_pages, cmp_step, jnp.zeros((2 * d, ncp), F32))
    kvc = (kvc + jnp.mean(pe_ref[...], axis=1, keepdims=True)).astype(BF16)
    pc, oc = _nsa_compressed(q, kvc[0:d], kvc[d:2 * d], mcol, qpos, nc)
    psum = jnp.sum(pc.reshape(N_HEADS, n_tok, ncp), axis=0)
    imp2 = psum + pltpu.roll(psum, ncp - 1, axis=1)
    w = ncp + LANES
    imp2 = jnp.concatenate([imp2, jnp.zeros((n_tok, LANES), F32)], axis=1)
    sel = _tile_rows(_nsa_select(imp2, past + _iota((n_tok, 1), 0), ns), N_HEADS).astype(BF16)

    def sel_step(p, carry):
        m, l, acc = carry
        kt = kvbuf[slot, p, 2 * d:3 * d, :].astype(BF16)
        vt = kvbuf[slot, p, 3 * d:4 * d, :].astype(BF16)
        l_i = _iota((w, PAGE), 0)
        k_i = _iota((w, PAGE), 1) + p * PAGE
        e = jnp.where(l_i == 2 * (k_i // SEL_BLOCK), 1.0, 0.0).astype(BF16)
        chosen = _dot(sel, e)
        kpos = p * PAGE + lane
        s = _dot(q, kt) - mcol * (qf - kpos.astype(F32))
        return _osm_step(jnp.where(chosen > 0.5, s, NEG), m, l, acc, vt)

    init = (jnp.full((rows, 1), NEG, F32), jnp.zeros((rows, 1), F32), jnp.zeros((rows, d), F32))
    m, l, acc = lax.fori_loop(0, n_pages, sel_step, init)
    kpos = past + lane
    chosen = _dot(sel, _sel_expand_matrix(w, past, PAGE))
    s = _dot(q, new_ref[0, 2 * d:3 * d, :].astype(BF16)) - mcol * (qf - kpos.astype(F32))
    s = jnp.where((chosen > 0.5) & (kpos <= qpos), s, NEG)
    m, l, acc = _osm_step(s, m, l, acc, new_ref[0, 3 * d:4 * d, :].astype(BF16))
    o_sel = acc / l

    wb = win_ref.shape[3]
    wlane = _iota((1, wb), 1)
    dist = qpos - (past - wb + wlane)
    sw = _dot(q, win_ref[0, 0, 0:d, :].astype(BF16)) - mcol * dist.astype(F32)
    sw = jnp.where((dist >= 0) & (dist <= WINDOW), sw, NEG)
    dist_n = qpos - kpos
    sn = _dot(q, new_ref[0, 4 * d:5 * d, :].astype(BF16)) - mcol * dist_n.astype(F32)
    sn = jnp.where((dist_n >= 0) & (dist_n <= WINDOW), sn, NEG)
    mx = jnp.maximum(jnp.max(sw, axis=-1, keepdims=True), jnp.max(sn, axis=-1, keepdims=True))
    pw = jnp.exp(sw - mx)
    pn = jnp.exp(sn - mx)
    lw = jnp.sum(pw, axis=-1, keepdims=True) + jnp.sum(pn, axis=-1, keepdims=True)
    o_win = (_dot_nt(pw.astype(BF16), win_ref[0, 0, d:2 * d, :].astype(BF16))
             + _dot_nt(pn.astype(BF16), new_ref[0, 5 * d:6 * d, :].astype(BF16))) / lw
    g = gate_ref[0]
    o_ref[0] = g[:, 0:1] * oc + g[:, 1:2] * o_sel + g[:, 2:3] * o_win


def _nsa_decode(li, pt, q, gates, newt, pet, wint, kv_pool):
    s, rows, d = q.shape
    n_pages = pt.shape[1]
    past = n_pages * PAGE
    ncp = -(-(n_pages * (PAGE // CMP_BLOCK)) // LANES) * LANES
    wb = wint.shape[3]
    scratch = [pltpu.VMEM((2, n_pages, 4 * d, PAGE), F32), pltpu.SemaphoreType.DMA((2,))]
    in_specs = [_seq_spec(q.shape), _seq_spec(gates.shape), _seq_spec(newt.shape),
                pl.BlockSpec(pet.shape, lambda b, pt_ref: (0, 0)),
                pl.BlockSpec((1, 1, 2 * d, wb), lambda b, pt_ref: (li, b, 0, 0))]
    return _decode_call(functools.partial(_nsa_decode_body, li, past, ncp), "nsa_decode", pt,
                        (q, gates, newt, pet, wint), in_specs, (kv_pool,),
                        (1, rows, d), SDS((s, rows, d), F32), scratch)


def _pad_lanes(a, width=LANES):
    return jnp.pad(a, [(0, 0)] * (a.ndim - 1) + [(0, width - a.shape[-1])])


def _heads_first(a, b, t, h):
    return a.reshape(b, t, h, -1).transpose(0, 2, 1, 3)


def _rows_from_heads(o):
    b, h, t, w = o.shape
    return o.transpose(0, 2, 1, 3).reshape(b * t, h * w)


def _key_blocks(kvt, lo, hi):
    b, _, t = kvt.shape
    return kvt[:, lo:hi].reshape(b, hi - lo, t // TK, TK).transpose(0, 2, 1, 3)


def _group_key_blocks(kvt, lo, g):
    b, _, t = kvt.shape
    d = HEAD_DIM
    return kvt[:, lo:lo + g * d].reshape(b, g, d, t // TK, TK).transpose(0, 1, 3, 2, 4)


def _sample_rows(a, s, t, h):
    return a.reshape(s, t, h, -1).transpose(0, 2, 1, 3).reshape(s, h * t, -1)


def _sample_rows_back(o, t):
    s, rows, w = o.shape
    h = rows // t
    return o.reshape(s, h, t, w).transpose(0, 2, 1, 3).reshape(s * t, h * w)


def _new_pages(kvt, s, t):
    c = kvt.shape[1]
    return _pad_lanes(kvt[0].reshape(c, s, t).transpose(1, 0, 2))


def _even_weights(w_in, b_f):
    o = np.cumsum((0, 512, 384, 24, 512, 512, 256, 8, 512))
    small = _pad_lanes(jnp.concatenate([w_in[:, o[2]:o[3]], w_in[:, o[6]:o[7]]], axis=1))
    wn = jnp.concatenate([w_in[:, o[0]:o[1]], w_in[:, o[3]:o[4]], w_in[:, o[4]:o[5]], w_in[:, o[7]:o[8]], small],
                         axis=1).astype(BF16)
    wt = jnp.concatenate([w_in[:, o[1]:o[2]], w_in[:, o[5]:o[6]]], axis=1).T.astype(BF16)
    bs = jnp.zeros((1, LANES), F32).at[0, 3 * N_HEADS:3 * N_HEADS + N_HEADS].set(b_f)
    return wn, wt, bs


def _odd_weights(w_in, w_uq, w_uk, w_uv):
    o = np.cumsum((0, 512, 256, 512, 256, 128, 32, 512))
    wn = jnp.concatenate([w_in[:, o[0]:o[1]], w_in[:, o[2]:o[3]], w_in[:, o[6]:o[7]], w_in[:, o[3]:o[4]]],
                         axis=1).astype(BF16)
    wt = jnp.concatenate([w_in[:, o[1]:o[2]], w_in[:, o[4]:o[5]], w_in[:, o[5]:o[6]]], axis=1).T.astype(BF16)
    uq = w_uq.reshape(Q_LORA, N_HEADS, NOPE_DIM + ROPE_DIM)
    wuqn = uq[:, :, :NOPE_DIM].reshape(Q_LORA, N_HEADS * NOPE_DIM).astype(BF16)
    half = ROPE_DIM // 2
    wuqr = uq[:, :, NOPE_DIM:].reshape(Q_LORA, N_HEADS, 2, half).transpose(0, 2, 1, 3).reshape(Q_LORA, 2 * LANES)
    eye = jnp.eye(N_HEADS, dtype=F32)
    wuk = jnp.einsum('chn,hg->hngc', w_uk, eye).reshape(N_HEADS * NOPE_DIM, N_HEADS * KV_LORA).astype(BF16)
    wuv = jnp.einsum('chv,hg->hcgv', w_uv, eye).reshape(N_HEADS * KV_LORA, N_HEADS * V_DIM).astype(BF16)
    return wn, wt, wuqn, wuqr.astype(BF16), wuk, wuv


def _rope_tables(pos):
    half = ROPE_DIM // 2
    inv = ROPE_BASE ** (-jnp.arange(half, dtype=F32) / half)
    ang = pos.astype(F32)[:, None] * inv[None, :]
    cos, sin = jnp.cos(ang), jnp.sin(ang)
    rep = LANES // half
    return jnp.tile(cos, (1, rep)), jnp.tile(sin, (1, rep)), cos.T, sin.T


def _even_prompt(x2, b, g_pre, g_post, wts, pet, w_out):
    wn, wt, bs = wts
    t = x2.shape[0] // b
    h, d = N_HEADS, HEAD_DIM
    qn, zn, qf, zf, small, kvt, kvtb = _proj_even(x2, g_pre, wn, wt, bs, b)
    nc = t // CMP_BLOCK
    ncp = -(-nc // LANES) * LANES
    _, kvcb = _nsa_compress(kvt, pet, ncp)
    o_n = _nsa_prompt(_heads_first(qn, b, t, h), small, kvcb, _key_blocks(kvtb, 2 * d, 4 * d),
                      _key_blocks(kvtb, 4 * d, 6 * d), nc, t // SEL_BLOCK)
    logf = small[:, 3 * h:4 * h].reshape(b, t, h)
    cum = _cumsum_lanes(logf.transpose(0, 2, 1))
    r = h // FOX_G
    cumq = cum.reshape(b, FOX_G, r, t).transpose(0, 1, 3, 2)
    cumk = cum.reshape(b, FOX_G, r, t // TK, TK).transpose(0, 1, 3, 2, 4)
    o_f = _fox_prompt(_heads_first(qf, b, t, h), _group_key_blocks(kvtb, 6 * d, FOX_G),
                      _group_key_blocks(kvtb, 8 * d, FOX_G), cumq, cumk)
    x_new = _merge(False, _rows_from_heads(o_n), zn, _rows_from_heads(o_f), zf, w_out, w_out[:8, :128], g_post, x2)
    kv_rows = kvt.transpose(0, 2, 1)
    n_win = min(WINDOW, t)
    outs = (kv_rows[:, :, 0:4 * d].reshape(b, t, 4, 1, d),
            kv_rows[:, t - n_win:, 4 * d:6 * d].reshape(b, n_win, 2, 1, d),
            kv_rows[:, :, 6 * d:10 * d].reshape(b, t, 2, FOX_G, d), logf)
    return x_new, outs


def _even_sample(x2, s, li, pt, g_pre, g_post, wts, pet, w_out, nsa_pool, win_state, win_t, fox_pool, lf_pool):
    wn, wt, bs = wts
    t = x2.shape[0] // s
    h, d = N_HEADS, HEAD_DIM
    qn, zn, qf, zf, small, kvt, _ = _proj_even(x2, g_pre, wn, wt, bs, 1)
    gates = _sample_rows(small[:, 0:3 * h], s, t, h)
    newt = _new_pages(kvt, s, t)
    o_n = _nsa_decode(li, pt, _sample_rows(qn, s, t, h), gates, newt[:, 0:6 * d], pet, win_t, nsa_pool)
    logf = small[:, 3 * h:4 * h].reshape(s, t, h)
    lfnew = _pad_lanes(logf.transpose(0, 2, 1))
    lq = _pad_lanes(jnp.broadcast_to(logf.transpose(0, 2, 1)[:, :, None, :], (s, h, t, t)).reshape(s, h * t, t))
    o_f = _fox_decode(li, pt, _sample_rows(qf, s, t, h), newt[:, 6 * d:10 * d], lfnew, lq, fox_pool, lf_pool)
    x_new = _merge(False, _sample_rows_back(o_n, t), zn, _sample_rows_back(o_f, t), zf, w_out, w_out[:8, :128],
                   g_post, x2)
    kv_rows = kvt[0].T.reshape(s, t, 10 * d)
    new_win = kv_rows[:, :, 4 * d:6 * d].reshape(s, t, 2, 1, d)
    outs = (kv_rows[:, :, 0:4 * d].reshape(s, t, 4, 1, d),
            jnp.concatenate([win_state[li], new_win], axis=1)[:, t:],
            kv_rows[:, :, 6 * d:10 * d].reshape(s, t, 2, FOX_G, d), logf)
    return x_new, outs


def _odd_prompt(x2, b, g_pre, g_post, wts, qn, kvn, w_out):
    wn, wt, wuqn, wuqr, wuk, wuv = wts
    t = x2.shape[0] // b
    h, d = N_HEADS, HEAD_DIM
    cosq, sinq, cost, sint = _rope_tables(jnp.arange(t, dtype=jnp.int32))
    qs, zs, zm, ql, qr, kvt, kvtb, lat, latb = _proj_odd(x2, g_pre, wn, wt, qn, kvn, wuqn, wuqr, wuk,
                                                         cosq, sinq, cost, sint, b)
    o_s = _sb_prompt(_heads_first(qs, b, t, h), _group_key_blocks(kvtb, 0, SB_G), _group_key_blocks(kvtb, 2 * d, SB_G))
    half = ROPE_DIM // 2
    qr_h = qr.reshape(b, t, 2, h, half).transpose(0, 3, 1, 2, 4).reshape(b, h, t, ROPE_DIM)
    o_l = _mla_prompt(_heads_first(ql, b, t, h), qr_h, _key_blocks(latb, 0, KV_LORA), _key_blocks(latb, KV_LORA, LAT))
    x_new = _merge(True, _rows_from_heads(o_s), zs, _rows_from_heads(o_l), zm, w_out, wuv, g_post, x2)
    outs = (kvt.transpose(0, 2, 1).reshape(b, t, 2, SB_G, d), lat.transpose(0, 2, 1))
    return x_new, outs


def _odd_sample(x2, s, li, pt, g_pre, g_post, wts, qn, kvn, w_out, sb_pool, lat_pool):
    wn, wt, wuqn, wuqr, wuk, wuv = wts
    t = x2.shape[0] // s
    h, d = N_HEADS, HEAD_DIM
    past = pt.shape[1] * PAGE
    pos = jnp.tile(past + jnp.arange(t, dtype=jnp.int32), s)
    cosq, sinq, cost, sint = _rope_tables(pos)
    qs, zs, zm, ql, qr, kvt, _, lat, _ = _proj_odd(x2, g_pre, wn, wt, qn, kvn, wuqn, wuqr, wuk,
                                                   cosq, sinq, cost, sint, 1)
    o_s = _sb_decode(li, pt, _sample_rows(qs, s, t, h), _new_pages(kvt, s, t), sb_pool)
    half = ROPE_DIM // 2
    qr_h = qr.reshape(s, t, 2, h, half).transpose(0, 3, 1, 2, 4).reshape(s, h * t, ROPE_DIM)
    o_l = _mla_decode(li, pt, _sample_rows(ql, s, t, h), qr_h, _new_pages(lat, s, t), lat_pool)
    x_new = _merge(True, _sample_rows_back(o_s, t), zs, _sample_rows_back(o_l, t), zm, w_out, wuv, g_post, x2)
    outs = (kvt[0].T.reshape(s, t, 2, SB_G, d), lat[0].T.reshape(s, t, LAT))
    return x_new, outs


def _feature_major_pool(pool):
    nd = pool.ndim
    p = pool.transpose((0, 1) + tuple(range(3, nd)) + (2,))
    return p.reshape(p.shape[0], p.shape[1], -1, p.shape[-1])


def kernel(x_prompt, x_sample, cache_nsa_kv, state_nsa_win, cache_fox_kv, cache_fox_logf, cache_sb_kv, cache_mla_latent, page_table, norm_pre, norm_post, w_in_e, b_f, nsa_pe, w_out_e, w_in_o, mla_q_norm, mla_kv_norm, w_uq, w_uk, w_uv, w_out_o):
    b, t_p, _ = x_prompt.shape
    s, t_s, _ = x_sample.shape
    depth = norm_pre.shape[0]
    xp = x_prompt.reshape(b * t_p, D_MODEL)
    xs = x_sample.reshape(s * t_s, D_MODEL)
    nsa_pool = _feature_major_pool(cache_nsa_kv)
    fox_pool = _feature_major_pool(cache_fox_kv)
    lf_pool = _feature_major_pool(cache_fox_logf)
    sb_pool = _feature_major_pool(cache_sb_kv)
    lat_pool = _feature_major_pool(cache_mla_latent)
    win_t = _feature_major_pool(state_nsa_win)
    pt = page_table.astype(jnp.int32)
    outs_p = [[] for _ in range(6)]
    outs_s = [[] for _ in range(6)]
    for layer in range(depth):
        i = layer // 2
        g_pre = norm_pre[layer][None, :]
        g_post = norm_post[layer][None, :]
        if layer % 2 == 0:
            wts = _even_weights(w_in_e[i], b_f[i])
            pet = nsa_pe[i].transpose(0, 2, 1).reshape(2 * HEAD_DIM, CMP_BLOCK)
            w_out = w_out_e[i].astype(BF16)
            xp, o_p = _even_prompt(xp, b, g_pre, g_post, wts, pet, w_out)
            xs, o_s = _even_sample(xs, s, i, pt, g_pre, g_post, wts, pet, w_out, nsa_pool, state_nsa_win, win_t,
                                   fox_pool, lf_pool)
            for k in range(4):
                outs_p[k].append(o_p[k])
                outs_s[k].append(o_s[k])
        else:
            wts = _odd_weights(w_in_o[i], w_uq[i], w_uk[i], w_uv[i])
            qn = mla_q_norm[i][None, :]
            kvn = mla_kv_norm[i][:, None]
            w_out = w_out_o[i].astype(BF16)
            xp, o_p = _odd_prompt(xp, b, g_pre, g_post, wts, qn, kvn, w_out)
            xs, o_s = _odd_sample(xs, s, i, pt, g_pre, g_post, wts, qn, kvn, w_out, sb_pool, lat_pool)
            for k in range(2):
                outs_p[4 + k].append(o_p[k])
                outs_s[4 + k].append(o_s[k])
    res = [xp.reshape(b, t_p, D_MODEL), xs.reshape(s, t_s, D_MODEL)]
    for k in range(6):
        res.append(jnp.stack(outs_p[k]))
        res.append(jnp.stack(outs_s[k]))
    return tuple(res)
```

```python
import functools
import math

import numpy as np
import jax
import jax.numpy as jnp
from jax import lax
from jax.experimental import pallas as pl
from jax.experimental.pallas import tpu as pltpu

F32 = jnp.float32
BF16 = jnp.bfloat16
SDS = jax.ShapeDtypeStruct

D_MODEL = 1024
HEAD_DIM = 64
N_HEADS = 8
PAGE = 128
EPS = 1e-6
NEG = -1e30
CMP_BLOCK = 32
SEL_BLOCK = 64
N_SEL = 16
WINDOW = 512
FORCE_SCORE = 1e4
FOX_G = 2
SB_G = 2
Q_LORA = 256
KV_LORA = 128
NOPE_DIM = 64
ROPE_DIM = 32
V_DIM = 64
ROPE_BASE = 10000.0
MLA_SCALE = 1.0 / math.sqrt(NOPE_DIM + ROPE_DIM)
LAT = KV_LORA + ROPE_DIM
QSCALE = 1.0 / math.sqrt(HEAD_DIM)

LANES = 128
TQ = 128
TK = 256
TK_MLA = 128
CHUNK = 8
VMEM_LIMIT = 56 * 1024 * 1024


def _params(n_axes, vmem=VMEM_LIMIT):
    return pltpu.CompilerParams(dimension_semantics=("arbitrary",) * n_axes, vmem_limit_bytes=vmem)


def _dot(a, b):
    return jnp.dot(a, b, preferred_element_type=F32)


def _dot_nt(a, b):
    return lax.dot_general(a, b, (((1,), (1,)), ((), ())), preferred_element_type=F32)


def _dot_split(x, w):
    hi = x.astype(BF16)
    lo = (x - hi.astype(F32)).astype(BF16)
    return _dot(hi, w) + _dot(lo, w)


def _rms(x, axis):
    return x * lax.rsqrt(jnp.mean(x * x, axis=axis, keepdims=True) + EPS)


def _sigmoid(x):
    return 1.0 / (1.0 + jnp.exp(-x))


def _softplus(x):
    return jnp.maximum(x, 0.0) + jnp.log1p(jnp.exp(-jnp.abs(x)))


def _silu(x):
    return x * _sigmoid(x)


def _iota(shape, axis):
    return lax.broadcasted_iota(jnp.int32, shape, axis)


def _lane_cumsum(x):
    lane = _iota(x.shape, 1)
    sh = 1
    while sh < x.shape[1]:
        x = x + jnp.where(lane >= sh, pltpu.roll(x, sh, axis=1), 0.0)
        sh *= 2
    return x


def _osm_step(s, m, l, acc, vt):
    m2 = jnp.maximum(m, jnp.max(s, axis=-1, keepdims=True))
    p = jnp.exp(s - m2)
    a = jnp.exp(m - m2)
    l2 = a * l + jnp.sum(p, axis=-1, keepdims=True)
    acc2 = a * acc + _dot_nt(p.astype(BF16), vt)
    return m2, l2, acc2


def _osm_chunk(s_list, m, l, acc, vt_list):
    mx = functools.reduce(jnp.maximum, s_list)
    m2 = jnp.maximum(m, jnp.max(mx, axis=-1, keepdims=True))
    a = jnp.exp(m - m2)
    ps = [jnp.exp(s - m2) for s in s_list]
    l2 = a * l + jnp.sum(functools.reduce(jnp.add, ps), axis=-1, keepdims=True)
    pv = functools.reduce(jnp.add, [_dot_nt(p.astype(BF16), vt) for p, vt in zip(ps, vt_list)])
    return m2, l2, a * acc + pv


def _rep_rows(x, n):
    return jnp.concatenate([jnp.broadcast_to(x[r:r + 1], (n, x.shape[1])) for r in range(x.shape[0])], axis=0)


def _tile_rows(x, n):
    return jnp.concatenate([x] * n, axis=0)


def _head_slopes_col(rows, per_head):
    h = _iota((rows, 1), 0) // per_head
    m = jnp.zeros((rows, 1), F32)
    for hh in range(N_HEADS):
        m = jnp.where(h == hh, 2.0 ** (-(hh + 1)), m)
    return m


def _proj_even_body(x_ref, g_ref, wn_ref, wt_ref, bs_ref,
                    qn_ref, zn_ref, qf_ref, zf_ref, sm_ref, kvt_ref, kvtb_ref):
    xb = (_rms(x_ref[...], -1) * g_ref[...]).astype(BF16)
    qn_ref[...] = (_dot(xb, wn_ref[:, 0:512]) * QSCALE).astype(BF16)
    zn_ref[...] = _dot(xb, wn_ref[:, 512:1024])
    qf_ref[...] = (_dot(xb, wn_ref[:, 1024:1536]) * QSCALE).astype(BF16)
    zf_ref[...] = _dot(xb, wn_ref[:, 1536:2048])
    s = _dot(xb, wn_ref[:, 2048:2176]) + bs_ref[...]
    lane = _iota(s.shape, 1)
    sm_ref[...] = jnp.where(lane < 3 * N_HEADS, _sigmoid(s), -_softplus(-s))
    ht = _dot_nt(wt_ref[...], xb)
    kvt_ref[0] = ht
    kvtb_ref[0] = ht.astype(BF16)


def _proj_even(x2, g, wn, wt, bs, nb, tm=256):
    n = x2.shape[0]
    tm = min(tm, n)
    t = n // nb
    nt = t // tm
    row = lambda i: (i, 0)
    const = lambda i: (0, 0)
    tr = lambda i: (i // nt, 0, i % nt)
    ct = wt.shape[0]
    return pl.pallas_call(
        _proj_even_body,
        grid=(n // tm,),
        in_specs=[pl.BlockSpec((tm, D_MODEL), row), pl.BlockSpec((1, D_MODEL), const),
                  pl.BlockSpec(wn.shape, const), pl.BlockSpec(wt.shape, const), pl.BlockSpec((1, LANES), const)],
        out_specs=[pl.BlockSpec((tm, 512), row), pl.BlockSpec((tm, 512), row), pl.BlockSpec((tm, 512), row),
                   pl.BlockSpec((tm, 512), row), pl.BlockSpec((tm, LANES), row),
                   pl.BlockSpec((1, ct, tm), tr), pl.BlockSpec((1, ct, tm), tr)],
        out_shape=[SDS((n, 512), BF16), SDS((n, 512), F32), SDS((n, 512), BF16), SDS((n, 512), F32),
                   SDS((n, LANES), F32), SDS((nb, ct, t), F32), SDS((nb, ct, t), BF16)],
        compiler_params=_params(1),
        name="proj_even",
    )(x2, g, wn, wt, bs)


def _proj_odd_body(x_ref, g_ref, wn_ref, wt_ref, qn_ref, kvn_ref, wuqn_ref, wuqr_ref, wuk_ref,
                   cq_ref, sq_ref, ct_ref, st_ref,
                   qs_ref, zs_ref, zm_ref, ql_ref, qr_ref, kvt_ref, kvtb_ref, lt_ref, ltb_ref):
    xb = (_rms(x_ref[...], -1) * g_ref[...]).astype(BF16)
    qs_ref[...] = (_dot(xb, wn_ref[:, 0:512]) * QSCALE).astype(BF16)
    zs_ref[...] = _dot(xb, wn_ref[:, 512:1024])
    zm_ref[...] = _dot(xb, wn_ref[:, 1024:1536])
    cq = _dot(xb, wn_ref[:, 1536:1792])
    cb = (_rms(cq, -1) * qn_ref[...]).astype(BF16)
    nope = _dot(cb, wuqn_ref[...]).astype(BF16)
    ql_ref[...] = _dot(nope, wuk_ref[...]).astype(BF16)
    rr = _dot(cb, wuqr_ref[...])
    x1, x2 = rr[:, 0:LANES], rr[:, LANES:2 * LANES]
    cq_t, sq_t = cq_ref[...], sq_ref[...]
    qr_ref[:, 0:LANES] = (x1 * cq_t - x2 * sq_t).astype(BF16)
    qr_ref[:, LANES:2 * LANES] = (x1 * sq_t + x2 * cq_t).astype(BF16)
    ht = _dot_nt(wt_ref[...], xb)
    kvt_ref[0] = ht[0:256]
    kvtb_ref[0] = ht[0:256].astype(BF16)
    ckv = _rms(ht[256:384], 0) * kvn_ref[...]
    half = ROPE_DIM // 2
    k1, k2 = ht[384:384 + half], ht[384 + half:384 + ROPE_DIM]
    c_t, s_t = ct_ref[...], st_ref[...]
    lat = jnp.concatenate([ckv, k1 * c_t - k2 * s_t, k1 * s_t + k2 * c_t], axis=0)
    lt_ref[0] = lat
    ltb_ref[0] = lat.astype(BF16)


def _proj_odd(x2, g, wn, wt, qn, kvn, wuqn, wuqr, wukbd, cosq, sinq, cost, sint, nb, tm=256):
    n = x2.shape[0]
    tm = min(tm, n)
    t = n // nb
    nt = t // tm
    row = lambda i: (i, 0)
    const = lambda i: (0, 0)
    tr = lambda i: (i // nt, 0, i % nt)
    prow = lambda i: (i % nt, 0)
    pcol = lambda i: (0, i % nt)
    half = ROPE_DIM // 2
    return pl.pallas_call(
        _proj_odd_body,
        grid=(n // tm,),
        in_specs=[pl.BlockSpec((tm, D_MODEL), row), pl.BlockSpec((1, D_MODEL), const),
                  pl.BlockSpec(wn.shape, const), pl.BlockSpec(wt.shape, const),
                  pl.BlockSpec((1, Q_LORA), const), pl.BlockSpec((KV_LORA, 1), const),
                  pl.BlockSpec(wuqn.shape, const), pl.BlockSpec(wuqr.shape, const), pl.BlockSpec(wukbd.shape, const),
                  pl.BlockSpec((tm, LANES), prow), pl.BlockSpec((tm, LANES), prow),
                  pl.BlockSpec((half, tm), pcol), pl.BlockSpec((half, tm), pcol)],
        out_specs=[pl.BlockSpec((tm, 512), row), pl.BlockSpec((tm, 512), row), pl.BlockSpec((tm, 512), row),
                   pl.BlockSpec((tm, 1024), row), pl.BlockSpec((tm, 256), row),
                   pl.BlockSpec((1, 256, tm), tr), pl.BlockSpec((1, 256, tm), tr),
                   pl.BlockSpec((1, LAT, tm), tr), pl.BlockSpec((1, LAT, tm), tr)],
        out_shape=[SDS((n, 512), BF16), SDS((n, 512), F32), SDS((n, 512), F32),
                   SDS((n, 1024), BF16), SDS((n, 256), BF16),
                   SDS((nb, 256, t), F32), SDS((nb, 256, t), BF16),
                   SDS((nb, LAT, t), F32), SDS((nb, LAT, t), BF16)],
        compiler_params=_params(1),
        name="proj_odd",
    )(x2, g, wn, wt, qn, kvn, wuqn, wuqr, wukbd, cosq, sinq, cost, sint)


def _merge_body(odd, oa_ref, za_ref, ob_ref, zb_ref, wo_ref, wuv_ref, gp_ref, x_ref, out_ref):
    a = (oa_ref[...] * _silu(za_ref[...])).astype(BF16)
    if odd:
        ob = _dot(ob_ref[...].astype(BF16), wuv_ref[...])
    else:
        ob = ob_ref[...]
    b = (ob * _silu(zb_ref[...])).astype(BF16)
    y = _dot(a, wo_ref[0:512, :]) + _dot(b, wo_ref[512:1024, :])
    out_ref[...] = x_ref[...] + _rms(y, -1) * gp_ref[...]


def _merge(odd, oa, za, ob, zb, wo, wuv, gp, x2, tm=256):
    n = x2.shape[0]
    tm = min(tm, n)
    row = lambda i: (i, 0)
    const = lambda i: (0, 0)
    return pl.pallas_call(
        functools.partial(_merge_body, odd),
        grid=(n // tm,),
        in_specs=[pl.BlockSpec((tm, 512), row), pl.BlockSpec((tm, 512), row),
                  pl.BlockSpec((tm, ob.shape[1]), row), pl.BlockSpec((tm, 512), row),
                  pl.BlockSpec(wo.shape, const), pl.BlockSpec(wuv.shape, const),
                  pl.BlockSpec((1, D_MODEL), const), pl.BlockSpec((tm, D_MODEL), row)],
        out_specs=pl.BlockSpec((tm, D_MODEL), row),
        out_shape=SDS((n, D_MODEL), F32),
        compiler_params=_params(1),
        name="merge_odd" if odd else "merge_even",
    )(oa, za, ob, zb, wo, wuv, gp, x2)


def _cumsum_body(lf_ref, out_ref):
    t = lf_ref.shape[2]
    carry = jnp.zeros((lf_ref.shape[1], 1), F32)
    for c in range(t // LANES):
        cs = _lane_cumsum(lf_ref[0, :, c * LANES:(c + 1) * LANES]) + carry
        out_ref[0, :, c * LANES:(c + 1) * LANES] = cs
        carry = cs[:, LANES - 1:LANES]


def _cumsum_lanes(lft):
    b, h, t = lft.shape
    spec = pl.BlockSpec((1, h, t), lambda i: (i, 0, 0))
    return pl.pallas_call(_cumsum_body, grid=(b,), in_specs=[spec], out_specs=spec,
                          out_shape=SDS(lft.shape, F32), compiler_params=_params(1), name="fox_cumsum")(lft)


def _causal_blocks(i, tk=TK):
    return (i * TQ + TQ - 1) // tk + 1


def _fox_prompt_body(q_ref, kt_ref, vt_ref, cq_ref, ck_ref, o_ref):
    i = pl.program_id(2)
    r_heads = q_ref.shape[1]
    qpos = i * TQ + _iota((TQ, 1), 0)
    lane = _iota((1, TK), 1)
    qs = [q_ref[0, r] for r in range(r_heads)]
    cqs = [cq_ref[0, 0][:, r:r + 1] for r in range(r_heads)]

    def step(j, carry):
        kt, vt, ck = kt_ref[0, 0, j], vt_ref[0, 0, j], ck_ref[0, 0, j]
        ok = j * TK + lane <= qpos
        out = []
        for r in range(r_heads):
            m, l, acc = carry[r]
            s = jnp.where(ok, _dot(qs[r], kt) + cqs[r] - ck[r:r + 1, :], NEG)
            out.append(_osm_step(s, m, l, acc, vt))
        return tuple(out)

    init = (jnp.full((TQ, 1), NEG, F32), jnp.zeros((TQ, 1), F32), jnp.zeros((TQ, HEAD_DIM), F32))
    res = lax.fori_loop(0, _causal_blocks(i), step, (init,) * r_heads)
    for r in range(r_heads):
        _, l, acc = res[r]
        o_ref[0, r] = acc / l


def _fox_prompt(q, ktb, vtb, cumq, cumk):
    b, h, t, d = q.shape
    g = ktb.shape[1]
    r = h // g
    nk = t // TK
    return pl.pallas_call(
        _fox_prompt_body,
        grid=(b, g, t // TQ),
        in_specs=[pl.BlockSpec((1, r, TQ, d), lambda bi, gi, i: (bi, gi, i, 0)),
                  pl.BlockSpec((1, 1, nk, d, TK), lambda bi, gi, i: (bi, gi, 0, 0, 0)),
                  pl.BlockSpec((1, 1, nk, d, TK), lambda bi, gi, i: (bi, gi, 0, 0, 0)),
                  pl.BlockSpec((1, 1, TQ, r), lambda bi, gi, i: (bi, gi, i, 0)),
                  pl.BlockSpec((1, 1, nk, r, TK), lambda bi, gi, i: (bi, gi, 0, 0, 0))],
        out_specs=pl.BlockSpec((1, r, TQ, d), lambda bi, gi, i: (bi, gi, i, 0)),
        out_shape=SDS((b, h, t, d), F32),
        compiler_params=_params(3),
        name="fox_prompt",
    )(q, ktb, vtb, cumq, cumk)


def _sb_block(z, strict, suf, acc, vt, u):
    sp = _softplus(z)
    ln = jnp.where(strict, -sp, 0.0)
    aft = _dot_split(ln, u) + suf
    a = jnp.where(strict, jnp.exp(z - sp + aft), 0.0)
    acc = acc + _dot_nt(a.astype(BF16), vt)
    return suf + jnp.sum(ln, axis=-1, keepdims=True), acc


def _sb_prompt_body(q_ref, kt_ref, vt_ref, u_ref, o_ref):
    i = pl.program_id(2)
    r_heads = q_ref.shape[1]
    qpos = i * TQ + _iota((TQ, 1), 0)
    lane = _iota((1, TK), 1)
    u = u_ref[...]
    qs = [q_ref[0, r] for r in range(r_heads)]
    nb = _causal_blocks(i)

    def step(jj, carry):
        j = nb - 1 - jj
        kt, vt = kt_ref[0, 0, j], vt_ref[0, 0, j]
        strict = j * TK + lane < qpos
        return tuple(_sb_block(_dot(qs[r], kt), strict, carry[r][0], carry[r][1], vt, u) for r in range(r_heads))

    init = (jnp.zeros((TQ, 1), F32), jnp.zeros((TQ, HEAD_DIM), F32))
    res = lax.fori_loop(0, nb, step, (init,) * r_heads)
    for r in range(r_heads):
        o_ref[0, r] = res[r][1]


def _suffix_matrix(n):
    a = np.arange(n)
    return jnp.asarray((a[:, None] > a[None, :]).astype(np.float32), BF16)


def _sb_prompt(q, ktb, vtb):
    b, h, t, d = q.shape
    g = ktb.shape[1]
    r = h // g
    nk = t // TK
    return pl.pallas_call(
        _sb_prompt_body,
        grid=(b, g, t // TQ),
        in_specs=[pl.BlockSpec((1, r, TQ, d), lambda bi, gi, i: (bi, gi, i, 0)),
                  pl.BlockSpec((1, 1, nk, d, TK), lambda bi, gi, i: (bi, gi, 0, 0, 0)),
                  pl.BlockSpec((1, 1, nk, d, TK), lambda bi, gi, i: (bi, gi, 0, 0, 0)),
                  pl.BlockSpec((TK, TK), lambda bi, gi, i: (0, 0))],
        out_specs=pl.BlockSpec((1, r, TQ, d), lambda bi, gi, i: (bi, gi, i, 0)),
        out_shape=SDS((b, h, t, d), F32),
        compiler_params=_params(3),
        name="sb_prompt",
    )(q, ktb, vtb, _suffix_matrix(TK))


def _mla_prompt_body(ql_ref, qr_ref, ck_ref, kr_ref, o_ref):
    i = pl.program_id(1)
    n_heads = ql_ref.shape[1]
    tk = ck_ref.shape[3]
    qpos = i * TQ + _iota((TQ, 1), 0)
    lane = _iota((1, tk), 1)
    qls = [ql_ref[0, h] for h in range(n_heads)]
    qrs = [qr_ref[0, h] for h in range(n_heads)]

    def step(j, carry):
        ck, kr = ck_ref[0, j], kr_ref[0, j]
        ok = j * tk + lane <= qpos
        out = []
        for h in range(n_heads):
            m, l, acc = carry[h]
            s = jnp.where(ok, (_dot(qls[h], ck) + _dot(qrs[h], kr)) * MLA_SCALE, NEG)
            out.append(_osm_step(s, m, l, acc, ck))
        return tuple(out)

    init = (jnp.full((TQ, 1), NEG, F32), jnp.zeros((TQ, 1), F32), jnp.zeros((TQ, KV_LORA), F32))
    res = lax.fori_loop(0, _causal_blocks(i, tk), step, (init,) * n_heads)
    for h in range(n_heads):
        _, l, acc = res[h]
        o_ref[0, h] = acc / l


def _mla_prompt(ql, qr, ckb, krb):
    b, h, t, c = ql.shape
    nk, tk = ckb.shape[1], ckb.shape[3]
    return pl.pallas_call(
        _mla_prompt_body,
        grid=(b, t // TQ),
        in_specs=[pl.BlockSpec((1, h, TQ, c), lambda bi, i: (bi, 0, i, 0)),
                  pl.BlockSpec((1, h, TQ, ROPE_DIM), lambda bi, i: (bi, 0, i, 0)),
                  pl.BlockSpec((1, nk, c, tk), lambda bi, i: (bi, 0, 0, 0)),
                  pl.BlockSpec((1, nk, ROPE_DIM, tk), lambda bi, i: (bi, 0, 0, 0))],
        out_specs=pl.BlockSpec((1, h, TQ, c), lambda bi, i: (bi, 0, i, 0)),
        out_shape=SDS((b, h, t, c), F32),
        compiler_params=_params(2),
        name="mla_prompt",
    )(ql, qr, ckb, krb)


def _nsa_compressed(q, kc, vct, mcol, qpos, n_real):
    ncp = kc.shape[1]
    lane = _iota((1, ncp), 1)
    c_end = (lane + 1) * CMP_BLOCK - 1
    c_mid = lane.astype(F32) * CMP_BLOCK + 0.5 * (CMP_BLOCK - 1)
    c_ok = (c_end <= qpos) & (lane < n_real)
    sc = _dot(q, kc) - mcol * (qpos.astype(F32) - c_mid)
    sc = jnp.where(c_ok, sc, NEG)
    e = jnp.exp(sc - jnp.max(sc, axis=-1, keepdims=True))
    pc = e / jnp.sum(e, axis=-1, keepdims=True) * c_ok.astype(F32)
    return pc, _dot_nt(pc.astype(BF16), vct)


def _nsa_select(imp2, qpos, ns):
    w = imp2.shape[1]
    lane = _iota((1, w), 1)
    blk = lane // 2
    is_blk = (lane % 2 == 0) & (blk < ns)
    cur = qpos // SEL_BLOCK
    valid = is_blk & (blk <= cur)
    forced = valid & ((blk == 0) | (blk == cur) | (blk == cur - 1))
    score = jnp.where(forced, FORCE_SCORE, jnp.where(valid, imp2, -1.0))
    score = jnp.where(is_blk, score, -jnp.inf)
    cnt = jnp.zeros(score.shape, F32)
    for k in range(ns):
        sk = score[:, 2 * k:2 * k + 1]
        ahead = (sk > score) | ((sk == score) & (2 * k < lane))
        cnt = cnt + ahead.astype(F32)
    return jnp.where(is_blk & (cnt < min(N_SEL, ns)), 1.0, 0.0)


def _sel_expand_matrix(w, first_key, n_keys):
    l = _iota((w, n_keys), 0)
    k = _iota((w, n_keys), 1) + first_key
    return jnp.where(l == 2 * (k // SEL_BLOCK), 1.0, 0.0).astype(BF16)


def _nsa_compress_body(cmp_ref, pe_ref, pool_ref, out_ref, outb_ref):
    pe_mean = jnp.mean(pe_ref[...], axis=1, keepdims=True)
    kvc = _dot_split(cmp_ref[0], pool_ref[...]) + pe_mean
    out_ref[0] = kvc
    outb_ref[0] = kvc.astype(BF16)


def _nsa_compress(kvt, pet, ncp):
    b, _, t = kvt.shape
    a = np.arange(t)[:, None] // CMP_BLOCK == np.arange(ncp)[None, :]
    pool = jnp.asarray(a.astype(np.float32) / CMP_BLOCK, BF16)
    return pl.pallas_call(
        _nsa_compress_body,
        grid=(b,),
        in_specs=[pl.BlockSpec((1, 2 * HEAD_DIM, t), lambda i: (i, 0, 0)),
                  pl.BlockSpec(pet.shape, lambda i: (0, 0)), pl.BlockSpec(pool.shape, lambda i: (0, 0))],
        out_specs=[pl.BlockSpec((1, 2 * HEAD_DIM, ncp), lambda i: (i, 0, 0)),
                   pl.BlockSpec((1, 2 * HEAD_DIM, ncp), lambda i: (i, 0, 0))],
        out_shape=[SDS((b, 2 * HEAD_DIM, ncp), F32), SDS((b, 2 * HEAD_DIM, ncp), BF16)],
        compiler_params=_params(1),
        name="nsa_compress",
    )(kvt, pet, pool)


def _nsa_prompt_body(nc, ns, q_ref, sm_ref, kvc_ref, sel_ref, win_ref, o_ref, mask_ref):
    i = pl.program_id(1)
    nk = sel_ref.shape[1]
    d = HEAD_DIM
    qpos = i * TQ + _iota((TQ, 1), 0)
    qf = qpos.astype(F32)
    lane = _iota((1, TK), 1)
    kc = kvc_ref[0, 0:d, :]
    vct = kvc_ref[0, d:2 * d, :]
    ncp = kc.shape[1]
    psum = jnp.zeros((TQ, ncp), F32)
    for h in range(N_HEADS):
        pc, oc = _nsa_compressed(q_ref[0, h], kc, vct, 2.0 ** (-(h + 1)), qpos, nc)
        psum = psum + pc
        o_ref[0, h] = sm_ref[:, 3 * h:3 * h + 1] * oc
    imp2 = psum + pltpu.roll(psum, ncp - 1, axis=1)
    sel = _nsa_select(imp2, qpos, ns).astype(BF16)
    for j in range(nk):
        mask_ref[j] = _dot(sel, _sel_expand_matrix(ncp, j * TK, TK))
    qs = [q_ref[0, h] for h in range(N_HEADS)]
    slopes = [2.0 ** (-(h + 1)) for h in range(N_HEADS)]

    def branch_step(kv_ref, j, dist, ok, carry):
        kt, vt = kv_ref[0, j, 0:d, :], kv_ref[0, j, d:2 * d, :]
        distf = dist.astype(F32)
        out = []
        for h in range(N_HEADS):
            m, l, acc = carry[h]
            s = jnp.where(ok, _dot(qs[h], kt) - slopes[h] * distf, NEG)
            out.append(_osm_step(s, m, l, acc, vt))
        return tuple(out)

    def sel_step(j, carry):
        dist = qpos - (j * TK + lane)
        return branch_step(sel_ref, j, dist, (mask_ref[j] > 0.5) & (dist >= 0), carry)

    def win_step(j, carry):
        dist = qpos - (j * TK + lane)
        return branch_step(win_ref, j, dist, (dist >= 0) & (dist <= WINDOW), carry)

    init = ((jnp.full((TQ, 1), NEG, F32), jnp.zeros((TQ, 1), F32), jnp.zeros((TQ, d), F32)),) * N_HEADS
    nb = _causal_blocks(i)
    lo = jnp.maximum(i * TQ - WINDOW, 0) // TK
    res_s = lax.fori_loop(0, nb, sel_step, init)
    res_w = lax.fori_loop(lo, nb, win_step, init)
    for h in range(N_HEADS):
        g1 = sm_ref[:, 3 * h + 1:3 * h + 2]
        g2 = sm_ref[:, 3 * h + 2:3 * h + 3]
        o_ref[0, h] = o_ref[0, h] + g1 * (res_s[h][2] / res_s[h][1]) + g2 * (res_w[h][2] / res_w[h][1])


def _nsa_prompt(q, small, kvcb, selb, winb, nc, ns):
    b, h, t, d = q.shape
    nk = t // TK
    nq = t // TQ
    ncp = kvcb.shape[2]
    return pl.pallas_call(
        functools.partial(_nsa_prompt_body, nc, ns),
        grid=(b, nq),
        in_specs=[pl.BlockSpec((1, h, TQ, d), lambda bi, i: (bi, 0, i, 0)),
                  pl.BlockSpec((TQ, LANES), lambda bi, i: (bi * nq + i, 0)),
                  pl.BlockSpec((1, 2 * d, ncp), lambda bi, i: (bi, 0, 0)),
                  pl.BlockSpec((1, nk, 2 * d, TK), lambda bi, i: (bi, 0, 0, 0)),
                  pl.BlockSpec((1, nk, 2 * d, TK), lambda bi, i: (bi, 0, 0, 0))],
        out_specs=pl.BlockSpec((1, h, TQ, d), lambda bi, i: (bi, 0, i, 0)),
        out_shape=SDS((b, h, t, d), F32),
        scratch_shapes=[pltpu.VMEM((nk, TQ, TK), F32)],
        compiler_params=_params(2),
        name="nsa_prompt",
    )(q, small, kvcb, selb, winb)


def _own_group_first(o):
    rows, gd = o.shape
    return jnp.where(_iota((rows, 1), 0) < rows // 2, o, pltpu.roll(o, gd // 2, axis=1))


def _fetch(pool, li, pt_ref, buf, sem, seq, slot, start):
    n_pages = buf.shape[1]

    def body(p, c):
        page = pt_ref[seq, p] if start else 0
        cp = pltpu.make_async_copy(pool.at[li, page], buf.at[slot, p], sem.at[slot])
        if start:
            cp.start()
        else:
            cp.wait()
        return c

    lax.fori_loop(0, n_pages, body, 0)


def _pipeline_pages(pools, li, pt_ref, bufs, sems):
    b = pl.program_id(0)
    nb = pl.num_programs(0)
    slot = b % 2

    @pl.when(b == 0)
    def _():
        for pool, buf, sem in zip(pools, bufs, sems):
            _fetch(pool, li, pt_ref, buf, sem, 0, 0, True)

    @pl.when(b + 1 < nb)
    def _():
        for pool, buf, sem in zip(pools, bufs, sems):
            _fetch(pool, li, pt_ref, buf, sem, b + 1, 1 - slot, True)

    for pool, buf, sem in zip(pools, bufs, sems):
        _fetch(pool, li, pt_ref, buf, sem, b, slot, False)
    return slot


def _fox_decode_body(li, past, pt_ref, q_ref, new_ref, lfn_ref, lq_ref, kv_hbm, lf_hbm, o_ref,
                     kvbuf, lfbuf, cum_ref, sem_kv, sem_lf):
    slot = _pipeline_pages((kv_hbm, lf_hbm), li, pt_ref, (kvbuf, lfbuf), (sem_kv, sem_lf))
    n_pages = kvbuf.shape[1]
    gd = FOX_G * HEAD_DIM
    rows = q_ref.shape[1]
    n_tok = rows // N_HEADS
    ch = min(CHUNK, n_pages)

    cs = _lane_cumsum(lfbuf[slot].reshape(n_pages * N_HEADS, PAGE))
    tot = jnp.broadcast_to(cs[:, PAGE - 1:PAGE], cs.shape).reshape(n_pages, N_HEADS, PAGE)
    inc = tot
    sh = 1
    while sh < n_pages:
        inc = inc + jnp.concatenate([jnp.zeros((sh, N_HEADS, PAGE), F32), inc[:n_pages - sh]], axis=0)
        sh *= 2
    cum_ref[...] = cs.reshape(n_pages, N_HEADS, PAGE) + (inc - tot) - inc[n_pages - 1]
    cum_new = _lane_cumsum(lfn_ref[0])
    lane = _iota((1, LANES), 1)
    tok = _iota((rows, 1), 0) % n_tok
    q = q_ref[0]
    cq = jnp.sum(jnp.where(lane <= tok, lq_ref[0], 0.0), axis=-1, keepdims=True)

    def step(c, carry):
        m, l, acc = carry
        ss, vts = [], []
        for k in range(ch):
            p = c * ch + k
            ss.append(_dot(q, kvbuf[slot, p, 0:gd, :].astype(BF16)) + cq - _rep_rows(cum_ref[p], n_tok))
            vts.append(kvbuf[slot, p, gd:2 * gd, :].astype(BF16))
        return _osm_chunk(ss, m, l, acc, vts)

    init = (jnp.full((rows, 1), NEG, F32), jnp.zeros((rows, 1), F32), jnp.zeros((rows, gd), F32))
    m, l, acc = lax.fori_loop(0, n_pages // ch, step, init)
    s = _dot(q, new_ref[0, 0:gd, :].astype(BF16)) + cq - _rep_rows(cum_new, n_tok)
    m, l, acc = _osm_step(jnp.where(lane <= tok, s, NEG), m, l, acc, new_ref[0, gd:2 * gd, :].astype(BF16))
    o_ref[0] = _own_group_first(acc / l)


def _decode_call(body, name, pt, ins, in_specs, pools, out_block, out_shape, scratch):
    nb = pt.shape[0]
    any_spec = pl.BlockSpec(memory_space=pl.ANY)
    grid_spec = pltpu.PrefetchScalarGridSpec(
        num_scalar_prefetch=1, grid=(nb,),
        in_specs=list(in_specs) + [any_spec] * len(pools),
        out_specs=pl.BlockSpec(out_block, lambda b, pt_ref: (b, 0, 0)),
        scratch_shapes=scratch)
    return pl.pallas_call(body, grid_spec=grid_spec, out_shape=out_shape,
                          compiler_params=_params(1), name=name)(pt, *ins, *pools)


def _seq_spec(shape):
    return pl.BlockSpec((1,) + tuple(shape[1:]), lambda b, pt_ref: (b,) + (0,) * (len(shape) - 1))


def _fox_decode(li, pt, q, newt, lfnew, lq, kv_pool, lf_pool):
    s, rows, d = q.shape
    n_pages = pt.shape[1]
    past = n_pages * PAGE
    scratch = [pltpu.VMEM((2, n_pages, 2 * d, PAGE), F32), pltpu.VMEM((2, n_pages, N_HEADS, PAGE), F32),
               pltpu.VMEM((n_pages, N_HEADS, PAGE), F32),
               pltpu.SemaphoreType.DMA((2,)), pltpu.SemaphoreType.DMA((2,))]
    ins = (q, newt, lfnew, lq)
    return _decode_call(functools.partial(_fox_decode_body, li, past), "fox_decode", pt, ins,
                        [_seq_spec(a.shape) for a in ins], (kv_pool, lf_pool),
                        (1, rows, d), SDS((s, rows, d), F32), scratch)


def _sb_decode_body(li, past, pt_ref, q_ref, new_ref, u_ref, kv_hbm, o_ref, kvbuf, sem_kv):
    slot = _pipeline_pages((kv_hbm,), li, pt_ref, (kvbuf,), (sem_kv,))
    n_pages = kvbuf.shape[1]
    gd = SB_G * HEAD_DIM
    rows = q_ref.shape[1]
    n_tok = rows // N_HEADS
    ch = min(CHUNK, n_pages)
    n_chunks = n_pages // ch
    lane = _iota((1, LANES), 1)
    tok = _iota((rows, 1), 0) % n_tok
    u = u_ref[...]
    q = q_ref[0]
    carry = _sb_block(_dot(q, new_ref[0, 0:gd, :].astype(BF16)), lane < tok, jnp.zeros((rows, 1), F32),
                      jnp.zeros((rows, gd), F32), new_ref[0, gd:2 * gd, :].astype(BF16), u)

    def step(cc, carry):
        suf, acc = carry
        c = n_chunks - 1 - cc
        zs, sps, vts, sums = [], [], [], []
        for k in range(ch):
            p = c * ch + k
            z = _dot(q, kvbuf[slot, p, 0:gd, :].astype(BF16))
            sp = _softplus(z)
            zs.append(z)
            sps.append(sp)
            vts.append(kvbuf[slot, p, gd:2 * gd, :].astype(BF16))
            sums.append(jnp.sum(sp, axis=-1, keepdims=True))
        for k in reversed(range(ch)):
            aft = suf - _dot_split(sps[k], u)
            acc = acc + _dot_nt(jnp.exp(zs[k] - sps[k] + aft).astype(BF16), vts[k])
            suf = suf - sums[k]
        return suf, acc

    _, acc = lax.fori_loop(0, n_chunks, step, carry)
    o_ref[0] = _own_group_first(acc)


def _sb_decode(li, pt, q, newt, kv_pool):
    s, rows, d = q.shape
    n_pages = pt.shape[1]
    scratch = [pltpu.VMEM((2, n_pages, 2 * d, PAGE), F32), pltpu.SemaphoreType.DMA((2,))]
    u = _suffix_matrix(PAGE)
    in_specs = [_seq_spec(q.shape), _seq_spec(newt.shape), pl.BlockSpec((PAGE, PAGE), lambda b, pt_ref: (0, 0))]
    return _decode_call(functools.partial(_sb_decode_body, li, n_pages * PAGE), "sb_decode", pt, (q, newt, u),
                        in_specs, (kv_pool,), (1, rows, d), SDS((s, rows, d), F32), scratch)


def _mla_decode_body(li, past, pt_ref, ql_ref, qr_ref, new_ref, lat_hbm, o_ref, buf, sem):
    slot = _pipeline_pages((lat_hbm,), li, pt_ref, (buf,), (sem,))
    n_pages = buf.shape[1]
    rows = ql_ref.shape[1]
    n_tok = rows // N_HEADS
    lane = _iota((1, LANES), 1)
    tok = _iota((rows, 1), 0) % n_tok
    ql = ql_ref[0]
    qr = qr_ref[0]

    ch = min(CHUNK, n_pages)

    def step(c, carry):
        m, l, acc = carry
        ss, cks = [], []
        for k in range(ch):
            p = c * ch + k
            ck = buf[slot, p, 0:KV_LORA, :].astype(BF16)
            kr = buf[slot, p, KV_LORA:LAT, :].astype(BF16)
            ss.append((_dot(ql, ck) + _dot(qr, kr)) * MLA_SCALE)
            cks.append(ck)
        return _osm_chunk(ss, m, l, acc, cks)

    init = (jnp.full((rows, 1), NEG, F32), jnp.zeros((rows, 1), F32), jnp.zeros((rows, KV_LORA), F32))
    m, l, acc = lax.fori_loop(0, n_pages // ch, step, init)
    ck = new_ref[0, 0:KV_LORA, :].astype(BF16)
    kr = new_ref[0, KV_LORA:LAT, :].astype(BF16)
    s = (_dot(ql, ck) + _dot(qr, kr)) * MLA_SCALE
    m, l, acc = _osm_step(jnp.where(lane <= tok, s, NEG), m, l, acc, ck)
    o_ref[0] = acc / l


def _mla_decode(li, pt, ql, qr, newt, lat_pool):
    s, rows, c = ql.shape
    n_pages = pt.shape[1]
    scratch = [pltpu.VMEM((2, n_pages, LAT, PAGE), F32), pltpu.SemaphoreType.DMA((2,))]
    ins = (ql, qr, newt)
    return _decode_call(functools.partial(_mla_decode_body, li, n_pages * PAGE), "mla_decode", pt, ins,
                        [_seq_spec(a.shape) for a in ins], (lat_pool,), (1, rows, c), SDS((s, rows, c), F32), scratch)


def _nsa_decode_body(li, past, ncp, pt_ref, q_ref, gate_ref, new_ref, pe_ref, win_ref, pool_ref, kv_hbm, o_ref,
                     kvbuf, pm_ref, sem_kv):
    slot = _pipeline_pages((kv_hbm,), li, pt_ref, (kvbuf,), (sem_kv,))
    n_pages = kvbuf.shape[1]
    d = HEAD_DIM
    rows = q_ref.shape[1]
    n_tok = rows // N_HEADS
    per_page = PAGE // CMP_BLOCK
    nc = n_pages * per_page
    ns = (past + n_tok + SEL_BLOCK - 1) // SEL_BLOCK
    q = q_ref[0]
    lane = _iota((1, LANES), 1)
    tok = _iota((rows, 1), 0) % n_tok
    qpos = past + tok
    qf = qpos.astype(F32)
    mcol = _head_slopes_col(rows, n_tok)

    per_tile = LANES // per_page
    tiles = []
    ch = min(CHUNK, n_pages)
    for t0 in range(0, n_pages, per_tile):
        def cmp_step(c, acc, t0=t0):
            parts = [_dot(kvbuf[slot, t0 + c * ch + k, 0:2 * d, :].astype(BF16), pool_ref[c * ch + k])
                     for k in range(ch)]
            return acc + functools.reduce(jnp.add, parts)

        tiles.append(lax.fori_loop(0, min(per_tile, n_pages - t0) // ch, cmp_step, jnp.zeros((2 * d, LANES), F32)))
    kvc = tiles[0] if len(tiles) == 1 else jnp.concatenate(tiles, axis=1)
    kvc = (kvc + jnp.mean(pe_ref[...], axis=1, keepdims=True)).astype(BF16)
    pc, oc = _nsa_compressed(q, kvc[0:d], kvc[d:2 * d], mcol, qpos, nc)
    psum = jnp.sum(pc.reshape(N_HEADS, n_tok, ncp), axis=0)
    imp2 = psum + pltpu.roll(psum, ncp - 1, axis=1)
    imp2 = jnp.concatenate([imp2, jnp.zeros((n_tok, LANES), F32)], axis=1)
    sel = _nsa_select(imp2, past + _iota((n_tok, 1), 0), ns)
    per_sel = PAGE // SEL_BLOCK
    for p in range(n_pages + 1):
        pm = sel[:, 2 * per_sel * p:2 * per_sel * p + 1]
        for b2 in range(1, per_sel):
            pm = jnp.where(lane < b2 * SEL_BLOCK, pm, sel[:, 2 * (per_sel * p + b2):2 * (per_sel * p + b2) + 1])
        pm_ref[p] = jnp.broadcast_to(pm, (n_tok, LANES))

    def sel_step(c, carry):
        m, l, acc = carry
        ss, vts = [], []
        for k in range(ch):
            p = c * ch + k
            kpos = p * PAGE + lane
            s = _dot(q, kvbuf[slot, p, 2 * d:3 * d, :].astype(BF16)) - mcol * (qf - kpos.astype(F32))
            ss.append(jnp.where(_tile_rows(pm_ref[p], N_HEADS) > 0.5, s, NEG))
            vts.append(kvbuf[slot, p, 3 * d:4 * d, :].astype(BF16))
        return _osm_chunk(ss, m, l, acc, vts)

    kpos = past + lane
    s = _dot(q, new_ref[0, 2 * d:3 * d, :].astype(BF16)) - mcol * (qf - kpos.astype(F32))
    s = jnp.where((_tile_rows(pm_ref[n_pages], N_HEADS) > 0.5) & (kpos <= qpos), s, NEG)
    init = (jnp.full((rows, 1), NEG, F32), jnp.zeros((rows, 1), F32), jnp.zeros((rows, d), F32))
    carry = _osm_step(s, init[0], init[1], init[2], new_ref[0, 3 * d:4 * d, :].astype(BF16))
    m, l, acc = lax.fori_loop(0, n_pages // ch, sel_step, carry)
    o_sel = acc / l

    wb = win_ref.shape[3]
    wlane = _iota((1, wb), 1)
    dist = qpos - (past - wb + wlane)
    sw = _dot(q, win_ref[0, 0, 0:d, :].astype(BF16)) - mcol * dist.astype(F32)
    sw = jnp.where((dist >= 0) & (dist <= WINDOW), sw, NEG)
    dist_n = qpos - kpos
    sn = _dot(q, new_ref[0, 4 * d:5 * d, :].astype(BF16)) - mcol * dist_n.astype(F32)
    sn = jnp.where((dist_n >= 0) & (dist_n <= WINDOW), sn, NEG)
    mx = jnp.maximum(jnp.max(sw, axis=-1, keepdims=True), jnp.max(sn, axis=-1, keepdims=True))
    pw = jnp.exp(sw - mx)
    pn = jnp.exp(sn - mx)
    lw = jnp.sum(pw, axis=-1, keepdims=True) + jnp.sum(pn, axis=-1, keepdims=True)
    o_win = (_dot_nt(pw.astype(BF16), win_ref[0, 0, d:2 * d, :].astype(BF16))
             + _dot_nt(pn.astype(BF16), new_ref[0, 5 * d:6 * d, :].astype(BF16))) / lw
    g = gate_ref[0]
    o_ref[0] = g[:, 0:1] * oc + g[:, 1:2] * o_sel + g[:, 2:3] * o_win


def _nsa_decode(li, pt, q, gates, newt, pet, wint, kv_pool):
    s, rows, d = q.shape
    n_pages = pt.shape[1]
    past = n_pages * PAGE
    ncp = -(-(n_pages * (PAGE // CMP_BLOCK)) // LANES) * LANES
    wb = wint.shape[3]
    n_tok = rows // N_HEADS
    per_page = PAGE // CMP_BLOCK
    per_tile = LANES // per_page
    k_i, key, col = np.meshgrid(np.arange(per_tile), np.arange(PAGE), np.arange(LANES), indexing='ij')
    pool = jnp.asarray((col == k_i * per_page + key // CMP_BLOCK).astype(np.float32) / CMP_BLOCK, BF16)
    scratch = [pltpu.VMEM((2, n_pages, 4 * d, PAGE), F32), pltpu.VMEM((n_pages + 1, n_tok, LANES), F32),
               pltpu.SemaphoreType.DMA((2,))]
    in_specs = [_seq_spec(q.shape), _seq_spec(gates.shape), _seq_spec(newt.shape),
                pl.BlockSpec(pet.shape, lambda b, pt_ref: (0, 0)),
                pl.BlockSpec((1, 1, 2 * d, wb), lambda b, pt_ref: (li, b, 0, 0)),
                pl.BlockSpec(pool.shape, lambda b, pt_ref: (0, 0, 0))]
    return _decode_call(functools.partial(_nsa_decode_body, li, past, ncp), "nsa_decode", pt,
                        (q, gates, newt, pet, wint, pool), in_specs, (kv_pool,),
                        (1, rows, d), SDS((s, rows, d), F32), scratch)


def _pad_lanes(a, width=LANES):
    return jnp.pad(a, [(0, 0)] * (a.ndim - 1) + [(0, width - a.shape[-1])])


def _heads_first(a, b, t, h):
    return a.reshape(b, t, h, -1).transpose(0, 2, 1, 3)


def _rows_from_heads(o):
    b, h, t, w = o.shape
    return o.transpose(0, 2, 1, 3).reshape(b * t, h * w)


def _key_blocks(kvt, lo, hi, tk=TK):
    b, _, t = kvt.shape
    return kvt[:, lo:hi].reshape(b, hi - lo, t // tk, tk).transpose(0, 2, 1, 3)


def _group_key_blocks(kvt, lo, g):
    b, _, t = kvt.shape
    d = HEAD_DIM
    return kvt[:, lo:lo + g * d].reshape(b, g, d, t // TK, TK).transpose(0, 1, 3, 2, 4)


def _sample_rows(a, s, t, h):
    return a.reshape(s, t, h, -1).transpose(0, 2, 1, 3).reshape(s, h * t, -1)


def _sample_rows_back(o, t):
    s, rows, w = o.shape
    h = rows // t
    return o.reshape(s, h, t, w).transpose(0, 2, 1, 3).reshape(s * t, h * w)


def _group_slots(q, g):
    s, rows, d = q.shape
    rg = rows // g
    parts = [jnp.pad(q[:, k * rg:(k + 1) * rg], ((0, 0), (0, 0), (k * d, (g - 1 - k) * d))) for k in range(g)]
    return jnp.concatenate(parts, axis=1)


def _own_group(o, g):
    assert g == 2
    return o[:, :, :o.shape[2] // g]


def _new_pages(kvt, s, t):
    c = kvt.shape[1]
    return _pad_lanes(kvt[0].reshape(c, s, t).transpose(1, 0, 2))


def _even_weights(w_in, b_f):
    o = np.cumsum((0, 512, 384, 24, 512, 512, 256, 8, 512))
    small = _pad_lanes(jnp.concatenate([w_in[:, o[2]:o[3]], w_in[:, o[6]:o[7]]], axis=1))
    wn = jnp.concatenate([w_in[:, o[0]:o[1]], w_in[:, o[3]:o[4]], w_in[:, o[4]:o[5]], w_in[:, o[7]:o[8]], small],
                         axis=1).astype(BF16)
    wt = jnp.concatenate([w_in[:, o[1]:o[2]], w_in[:, o[5]:o[6]]], axis=1).T.astype(BF16)
    bs = jnp.zeros((1, LANES), F32).at[0, 3 * N_HEADS:3 * N_HEADS + N_HEADS].set(b_f)
    return wn, wt, bs


def _odd_weights(w_in, w_uq, w_uk, w_uv):
    o = np.cumsum((0, 512, 256, 512, 256, 128, 32, 512))
    wn = jnp.concatenate([w_in[:, o[0]:o[1]], w_in[:, o[2]:o[3]], w_in[:, o[6]:o[7]], w_in[:, o[3]:o[4]]],
                         axis=1).astype(BF16)
    wt = jnp.concatenate([w_in[:, o[1]:o[2]], w_in[:, o[4]:o[5]], w_in[:, o[5]:o[6]]], axis=1).T.astype(BF16)
    uq = w_uq.reshape(Q_LORA, N_HEADS, NOPE_DIM + ROPE_DIM)
    wuqn = uq[:, :, :NOPE_DIM].reshape(Q_LORA, N_HEADS * NOPE_DIM).astype(BF16)
    half = ROPE_DIM // 2
    wuqr = uq[:, :, NOPE_DIM:].reshape(Q_LORA, N_HEADS, 2, half).transpose(0, 2, 1, 3).reshape(Q_LORA, 2 * LANES)
    eye = jnp.eye(N_HEADS, dtype=F32)
    wuk = jnp.einsum('chn,hg->hngc', w_uk, eye).reshape(N_HEADS * NOPE_DIM, N_HEADS * KV_LORA).astype(BF16)
    wuv = jnp.einsum('chv,hg->hcgv', w_uv, eye).reshape(N_HEADS * KV_LORA, N_HEADS * V_DIM).astype(BF16)
    return wn, wt, wuqn, wuqr.astype(BF16), wuk, wuv


def _rope_tables(pos):
    half = ROPE_DIM // 2
    inv = ROPE_BASE ** (-jnp.arange(half, dtype=F32) / half)
    ang = pos.astype(F32)[:, None] * inv[None, :]
    cos, sin = jnp.cos(ang), jnp.sin(ang)
    rep = LANES // half
    return jnp.tile(cos, (1, rep)), jnp.tile(sin, (1, rep)), cos.T, sin.T


def _even_prompt(x2, b, g_pre, g_post, wts, pet, w_out):
    wn, wt, bs = wts
    t = x2.shape[0] // b
    h, d = N_HEADS, HEAD_DIM
    qn, zn, qf, zf, small, kvt, kvtb = _proj_even(x2, g_pre, wn, wt, bs, b)
    nc = t // CMP_BLOCK
    ncp = -(-nc // LANES) * LANES
    _, kvcb = _nsa_compress(kvt, pet, ncp)
    o_n = _nsa_prompt(_heads_first(qn, b, t, h), small, kvcb, _key_blocks(kvtb, 2 * d, 4 * d),
                      _key_blocks(kvtb, 4 * d, 6 * d), nc, t // SEL_BLOCK)
    logf = small[:, 3 * h:4 * h].reshape(b, t, h)
    cum = _cumsum_lanes(logf.transpose(0, 2, 1))
    r = h // FOX_G
    cumq = cum.reshape(b, FOX_G, r, t).transpose(0, 1, 3, 2)
    cumk = cum.reshape(b, FOX_G, r, t // TK, TK).transpose(0, 1, 3, 2, 4)
    o_f = _fox_prompt(_heads_first(qf, b, t, h), _group_key_blocks(kvtb, 6 * d, FOX_G),
                      _group_key_blocks(kvtb, 8 * d, FOX_G), cumq, cumk)
    x_new = _merge(False, _rows_from_heads(o_n), zn, _rows_from_heads(o_f), zf, w_out, w_out[:8, :128], g_post, x2)
    kv_rows = kvt.transpose(0, 2, 1)
    n_win = min(WINDOW, t)
    outs = (kv_rows[:, :, 0:4 * d].reshape(b, t, 4, 1, d),
            kv_rows[:, t - n_win:, 4 * d:6 * d].reshape(b, n_win, 2, 1, d),
            kv_rows[:, :, 6 * d:10 * d].reshape(b, t, 2, FOX_G, d), logf)
    return x_new, outs


def _even_sample(x2, s, li, pt, g_pre, g_post, wts, pet, w_out, nsa_pool, win_state, win_t, fox_pool, lf_pool):
    wn, wt, bs = wts
    t = x2.shape[0] // s
    h, d = N_HEADS, HEAD_DIM
    qn, zn, qf, zf, small, kvt, _ = _proj_even(x2, g_pre, wn, wt, bs, 1)
    gates = _sample_rows(small[:, 0:3 * h], s, t, h)
    newt = _new_pages(kvt, s, t)
    o_n = _nsa_decode(li, pt, _sample_rows(qn, s, t, h), gates, newt[:, 0:6 * d], pet, win_t, nsa_pool)
    logf = small[:, 3 * h:4 * h].reshape(s, t, h)
    lfnew = _pad_lanes(logf.transpose(0, 2, 1))
    lq = _pad_lanes(jnp.broadcast_to(logf.transpose(0, 2, 1)[:, :, None, :], (s, h, t, t)).reshape(s, h * t, t))
    o_f = _fox_decode(li, pt, _group_slots(_sample_rows(qf, s, t, h), FOX_G), newt[:, 6 * d:10 * d], lfnew, lq,
                      fox_pool, lf_pool)
    x_new = _merge(False, _sample_rows_back(o_n, t), zn, _sample_rows_back(_own_group(o_f, FOX_G), t), zf, w_out,
                   w_out[:8, :128],
                   g_post, x2)
    kv_rows = kvt[0].T.reshape(s, t, 10 * d)
    new_win = kv_rows[:, :, 4 * d:6 * d].reshape(s, t, 2, 1, d)
    outs = (kv_rows[:, :, 0:4 * d].reshape(s, t, 4, 1, d),
            jnp.concatenate([win_state[li], new_win], axis=1)[:, t:],
            kv_rows[:, :, 6 * d:10 * d].reshape(s, t, 2, FOX_G, d), logf)
    return x_new, outs


def _odd_prompt(x2, b, g_pre, g_post, wts, qn, kvn, w_out):
    wn, wt, wuqn, wuqr, wuk, wuv = wts
    t = x2.shape[0] // b
    h, d = N_HEADS, HEAD_DIM
    cosq, sinq, cost, sint = _rope_tables(jnp.arange(t, dtype=jnp.int32))
    qs, zs, zm, ql, qr, kvt, kvtb, lat, latb = _proj_odd(x2, g_pre, wn, wt, qn, kvn, wuqn, wuqr, wuk,
                                                         cosq, sinq, cost, sint, b)
    o_s = _sb_prompt(_heads_first(qs, b, t, h), _group_key_blocks(kvtb, 0, SB_G), _group_key_blocks(kvtb, 2 * d, SB_G))
    half = ROPE_DIM // 2
    qr_h = qr.reshape(b, t, 2, h, half).transpose(0, 3, 1, 2, 4).reshape(b, h, t, ROPE_DIM)
    o_l = _mla_prompt(_heads_first(ql, b, t, h), qr_h, _key_blocks(latb, 0, KV_LORA, TK_MLA),
                      _key_blocks(latb, KV_LORA, LAT, TK_MLA))
    x_new = _merge(True, _rows_from_heads(o_s), zs, _rows_from_heads(o_l), zm, w_out, wuv, g_post, x2)
    outs = (kvt.transpose(0, 2, 1).reshape(b, t, 2, SB_G, d), lat.transpose(0, 2, 1))
    return x_new, outs


def _odd_sample(x2, s, li, pt, g_pre, g_post, wts, qn, kvn, w_out, sb_pool, lat_pool):
    wn, wt, wuqn, wuqr, wuk, wuv = wts
    t = x2.shape[0] // s
    h, d = N_HEADS, HEAD_DIM
    past = pt.shape[1] * PAGE
    pos = jnp.tile(past + jnp.arange(t, dtype=jnp.int32), s)
    cosq, sinq, cost, sint = _rope_tables(pos)
    qs, zs, zm, ql, qr, kvt, _, lat, _ = _proj_odd(x2, g_pre, wn, wt, qn, kvn, wuqn, wuqr, wuk,
                                                   cosq, sinq, cost, sint, 1)
    o_s = _own_group(_sb_decode(li, pt, _group_slots(_sample_rows(qs, s, t, h), SB_G), _new_pages(kvt, s, t),
                                sb_pool), SB_G)
    half = ROPE_DIM // 2
    qr_h = qr.reshape(s, t, 2, h, half).transpose(0, 3, 1, 2, 4).reshape(s, h * t, ROPE_DIM)
    o_l = _mla_decode(li, pt, _sample_rows(ql, s, t, h), qr_h, _new_pages(lat, s, t), lat_pool)
    x_new = _merge(True, _sample_rows_back(o_s, t), zs, _sample_rows_back(o_l, t), zm, w_out, wuv, g_post, x2)
    outs = (kvt[0].T.reshape(s, t, 2, SB_G, d), lat[0].T.reshape(s, t, LAT))
    return x_new, outs


def _feature_major_pool(pool):
    nd = pool.ndim
    p = pool.transpose((0, 1) + tuple(range(3, nd)) + (2,))
    return p.reshape(p.shape[0], p.shape[1], -1, p.shape[-1])


def kernel(x_prompt, x_sample, cache_nsa_kv, state_nsa_win, cache_fox_kv, cache_fox_logf, cache_sb_kv, cache_mla_latent, page_table, norm_pre, norm_post, w_in_e, b_f, nsa_pe, w_out_e, w_in_o, mla_q_norm, mla_kv_norm, w_uq, w_uk, w_uv, w_out_o):
    b, t_p, _ = x_prompt.shape
    s, t_s, _ = x_sample.shape
    depth = norm_pre.shape[0]
    xp = x_prompt.reshape(b * t_p, D_MODEL)
    xs = x_sample.reshape(s * t_s, D_MODEL)
    nsa_pool = _feature_major_pool(cache_nsa_kv)
    fox_pool = _feature_major_pool(cache_fox_kv)
    lf_pool = _feature_major_pool(cache_fox_logf)
    sb_pool = _feature_major_pool(cache_sb_kv)
    lat_pool = _feature_major_pool(cache_mla_latent)
    win_t = _feature_major_pool(state_nsa_win)
    pt = page_table.astype(jnp.int32)
    outs_p = [[] for _ in range(6)]
    outs_s = [[] for _ in range(6)]
    for layer in range(depth):
        i = layer // 2
        g_pre = norm_pre[layer][None, :]
        g_post = norm_post[layer][None, :]
        if layer % 2 == 0:
            wts = _even_weights(w_in_e[i], b_f[i])
            pet = nsa_pe[i].transpose(0, 2, 1).reshape(2 * HEAD_DIM, CMP_BLOCK)
            w_out = w_out_e[i].astype(BF16)
            xp, o_p = _even_prompt(xp, b, g_pre, g_post, wts, pet, w_out)
            xs, o_s = _even_sample(xs, s, i, pt, g_pre, g_post, wts, pet, w_out, nsa_pool, state_nsa_win, win_t,
                                   fox_pool, lf_pool)
            for k in range(4):
                outs_p[k].append(o_p[k])
                outs_s[k].append(o_s[k])
        else:
            wts = _odd_weights(w_in_o[i], w_uq[i], w_uk[i], w_uv[i])
            qn = mla_q_norm[i][None, :]
            kvn = mla_kv_norm[i][:, None]
            w_out = w_out_o[i].astype(BF16)
            xp, o_p = _odd_prompt(xp, b, g_pre, g_post, wts, qn, kvn, w_out)
            xs, o_s = _odd_sample(xs, s, i, pt, g_pre, g_post, wts, qn, kvn, w_out, sb_pool, lat_pool)
            for k in range(2):
                outs_p[4 + k].append(o_p[k])
                outs_s[4 + k].append(o_s[k])
    res = [xp.reshape(b, t_p, D_MODEL), xs.reshape(s, t_s, D_MODEL)]
    for k in range(6):
        res.append(jnp.stack(outs_p[k]))
        res.append(jnp.stack(outs_s[k]))
    return tuple(res)
```

```python
import functools
import math

import numpy as np
import jax
import jax.numpy as jnp
from jax import lax
from jax.experimental import pallas as pl
from jax.experimental.pallas import tpu as pltpu

F32 = jnp.float32
BF16 = jnp.bfloat16
SDS = jax.ShapeDtypeStruct

D_MODEL = 1024
HEAD_DIM = 64
N_HEADS = 8
PAGE = 128
EPS = 1e-6
NEG = -1e30
CMP_BLOCK = 32
SEL_BLOCK = 64
N_SEL = 16
WINDOW = 512
FORCE_SCORE = 1e4
FOX_G = 2
SB_G = 2
Q_LORA = 256
KV_LORA = 128
NOPE_DIM = 64
ROPE_DIM = 32
V_DIM = 64
ROPE_BASE = 10000.0
MLA_SCALE = 1.0 / math.sqrt(NOPE_DIM + ROPE_DIM)
LAT = KV_LORA + ROPE_DIM
QSCALE = 1.0 / math.sqrt(HEAD_DIM)

LANES = 128
TQ = 128
TK = 512
TK_NSA = 256
TK_MLA = 128
CHUNK = 32
VMEM_LIMIT = 56 * 1024 * 1024


def _params(n_axes, vmem=VMEM_LIMIT):
    return pltpu.CompilerParams(dimension_semantics=("arbitrary",) * n_axes, vmem_limit_bytes=vmem)


def _dot(a, b):
    return jnp.dot(a, b, preferred_element_type=F32)


def _dot_nt(a, b):
    return lax.dot_general(a, b, (((1,), (1,)), ((), ())), preferred_element_type=F32)


def _dot_split(x, w):
    hi = x.astype(BF16)
    lo = (x - hi.astype(F32)).astype(BF16)
    return _dot(hi, w) + _dot(lo, w)


def _rms(x, axis):
    return x * lax.rsqrt(jnp.mean(x * x, axis=axis, keepdims=True) + EPS)


def _sigmoid(x):
    return 1.0 / (1.0 + jnp.exp(-x))


def _softplus(x):
    return jnp.maximum(x, 0.0) + jnp.log1p(jnp.exp(-jnp.abs(x)))


def _silu(x):
    return x * _sigmoid(x)


def _iota(shape, axis):
    return lax.broadcasted_iota(jnp.int32, shape, axis)


def _lane_cumsum(x):
    lane = _iota(x.shape, 1)
    sh = 1
    while sh < x.shape[1]:
        x = x + jnp.where(lane >= sh, pltpu.roll(x, sh, axis=1), 0.0)
        sh *= 2
    return x


def _osm_step(s, m, l, acc, vt):
    m2 = jnp.maximum(m, jnp.max(s, axis=-1, keepdims=True))
    p = jnp.exp(s - m2)
    a = jnp.exp(m - m2)
    l2 = a * l + jnp.sum(p, axis=-1, keepdims=True)
    acc2 = a * acc + _dot_nt(p.astype(BF16), vt)
    return m2, l2, acc2


def _osm_chunk(s_list, m, l, acc, vt_list):
    mx = functools.reduce(jnp.maximum, s_list)
    m2 = jnp.maximum(m, jnp.max(mx, axis=-1, keepdims=True))
    a = jnp.exp(m - m2)
    ps = [jnp.exp(s - m2) for s in s_list]
    l2 = a * l + jnp.sum(functools.reduce(jnp.add, ps), axis=-1, keepdims=True)
    pv = functools.reduce(jnp.add, [_dot_nt(p.astype(BF16), vt) for p, vt in zip(ps, vt_list)])
    return m2, l2, a * acc + pv


def _rep_rows(x, n):
    return jnp.concatenate([jnp.broadcast_to(x[r:r + 1], (n, x.shape[1])) for r in range(x.shape[0])], axis=0)


def _tile_rows(x, n):
    return jnp.concatenate([x] * n, axis=0)


def _head_slopes_col(rows, per_head):
    h = _iota((rows, 1), 0) // per_head
    m = jnp.zeros((rows, 1), F32)
    for hh in range(N_HEADS):
        m = jnp.where(h == hh, 2.0 ** (-(hh + 1)), m)
    return m


def _proj_even_body(x_ref, g_ref, wn_ref, wt_ref, bs_ref,
                    qn_ref, zn_ref, qf_ref, zf_ref, sm_ref, kvt_ref, kvtb_ref):
    xb = (_rms(x_ref[...], -1) * g_ref[...]).astype(BF16)
    qn_ref[...] = (_dot(xb, wn_ref[:, 0:512]) * QSCALE).astype(BF16)
    zn_ref[...] = _dot(xb, wn_ref[:, 512:1024])
    qf_ref[...] = (_dot(xb, wn_ref[:, 1024:1536]) * QSCALE).astype(BF16)
    zf_ref[...] = _dot(xb, wn_ref[:, 1536:2048])
    s = _dot(xb, wn_ref[:, 2048:2176]) + bs_ref[...]
    lane = _iota(s.shape, 1)
    sm_ref[...] = jnp.where(lane < 3 * N_HEADS, _sigmoid(s), -_softplus(-s))
    ht = _dot_nt(wt_ref[...], xb)
    kvt_ref[0] = ht
    kvtb_ref[0] = ht.astype(BF16)


def _proj_even(x2, g, wn, wt, bs, nb, tm=256):
    n = x2.shape[0]
    tm = min(tm, n)
    t = n // nb
    nt = t // tm
    row = lambda i: (i, 0)
    const = lambda i: (0, 0)
    tr = lambda i: (i // nt, 0, i % nt)
    ct = wt.shape[0]
    return pl.pallas_call(
        _proj_even_body,
        grid=(n // tm,),
        in_specs=[pl.BlockSpec((tm, D_MODEL), row), pl.BlockSpec((1, D_MODEL), const),
                  pl.BlockSpec(wn.shape, const), pl.BlockSpec(wt.shape, const), pl.BlockSpec((1, LANES), const)],
        out_specs=[pl.BlockSpec((tm, 512), row), pl.BlockSpec((tm, 512), row), pl.BlockSpec((tm, 512), row),
                   pl.BlockSpec((tm, 512), row), pl.BlockSpec((tm, LANES), row),
                   pl.BlockSpec((1, ct, tm), tr), pl.BlockSpec((1, ct, tm), tr)],
        out_shape=[SDS((n, 512), BF16), SDS((n, 512), F32), SDS((n, 512), BF16), SDS((n, 512), F32),
                   SDS((n, LANES), F32), SDS((nb, ct, t), F32), SDS((nb, ct, t), BF16)],
        compiler_params=_params(1),
        name="proj_even",
    )(x2, g, wn, wt, bs)


def _proj_odd_body(x_ref, g_ref, wn_ref, wt_ref, qn_ref, kvn_ref, wuqn_ref, wuqr_ref, wuk_ref,
                   cq_ref, sq_ref, ct_ref, st_ref,
                   qs_ref, zs_ref, zm_ref, ql_ref, qr_ref, kvt_ref, kvtb_ref, lt_ref, ltb_ref):
    xb = (_rms(x_ref[...], -1) * g_ref[...]).astype(BF16)
    qs_ref[...] = (_dot(xb, wn_ref[:, 0:512]) * QSCALE).astype(BF16)
    zs_ref[...] = _dot(xb, wn_ref[:, 512:1024])
    zm_ref[...] = _dot(xb, wn_ref[:, 1024:1536])
    cq = _dot(xb, wn_ref[:, 1536:1792])
    cb = (_rms(cq, -1) * qn_ref[...]).astype(BF16)
    nope = _dot(cb, wuqn_ref[...]).astype(BF16)
    ql_ref[...] = _dot(nope, wuk_ref[...]).astype(BF16)
    rr = _dot(cb, wuqr_ref[...])
    x1, x2 = rr[:, 0:LANES], rr[:, LANES:2 * LANES]
    cq_t, sq_t = cq_ref[...], sq_ref[...]
    qr_ref[:, 0:LANES] = (x1 * cq_t - x2 * sq_t).astype(BF16)
    qr_ref[:, LANES:2 * LANES] = (x1 * sq_t + x2 * cq_t).astype(BF16)
    ht = _dot_nt(wt_ref[...], xb)
    kvt_ref[0] = ht[0:256]
    kvtb_ref[0] = ht[0:256].astype(BF16)
    ckv = _rms(ht[256:384], 0) * kvn_ref[...]
    half = ROPE_DIM // 2
    k1, k2 = ht[384:384 + half], ht[384 + half:384 + ROPE_DIM]
    c_t, s_t = ct_ref[...], st_ref[...]
    lat = jnp.concatenate([ckv, k1 * c_t - k2 * s_t, k1 * s_t + k2 * c_t], axis=0)
    lt_ref[0] = lat
    ltb_ref[0] = lat.astype(BF16)


def _proj_odd(x2, g, wn, wt, qn, kvn, wuqn, wuqr, wukbd, cosq, sinq, cost, sint, nb, tm=256):
    n = x2.shape[0]
    tm = min(tm, n)
    t = n // nb
    nt = t // tm
    row = lambda i: (i, 0)
    const = lambda i: (0, 0)
    tr = lambda i: (i // nt, 0, i % nt)
    prow = lambda i: (i % nt, 0)
    pcol = lambda i: (0, i % nt)
    half = ROPE_DIM // 2
    return pl.pallas_call(
        _proj_odd_body,
        grid=(n // tm,),
        in_specs=[pl.BlockSpec((tm, D_MODEL), row), pl.BlockSpec((1, D_MODEL), const),
                  pl.BlockSpec(wn.shape, const), pl.BlockSpec(wt.shape, const),
                  pl.BlockSpec((1, Q_LORA), const), pl.BlockSpec((KV_LORA, 1), const),
                  pl.BlockSpec(wuqn.shape, const), pl.BlockSpec(wuqr.shape, const), pl.BlockSpec(wukbd.shape, const),
                  pl.BlockSpec((tm, LANES), prow), pl.BlockSpec((tm, LANES), prow),
                  pl.BlockSpec((half, tm), pcol), pl.BlockSpec((half, tm), pcol)],
        out_specs=[pl.BlockSpec((tm, 512), row), pl.BlockSpec((tm, 512), row), pl.BlockSpec((tm, 512), row),
                   pl.BlockSpec((tm, 1024), row), pl.BlockSpec((tm, 256), row),
                   pl.BlockSpec((1, 256, tm), tr), pl.BlockSpec((1, 256, tm), tr),
                   pl.BlockSpec((1, LAT, tm), tr), pl.BlockSpec((1, LAT, tm), tr)],
        out_shape=[SDS((n, 512), BF16), SDS((n, 512), F32), SDS((n, 512), F32),
                   SDS((n, 1024), BF16), SDS((n, 256), BF16),
                   SDS((nb, 256, t), F32), SDS((nb, 256, t), BF16),
                   SDS((nb, LAT, t), F32), SDS((nb, LAT, t), BF16)],
        compiler_params=_params(1),
        name="proj_odd",
    )(x2, g, wn, wt, qn, kvn, wuqn, wuqr, wukbd, cosq, sinq, cost, sint)


def _merge_body(odd, oa_ref, za_ref, ob_ref, zb_ref, wo_ref, wuv_ref, gp_ref, x_ref, out_ref):
    a = (oa_ref[...] * _silu(za_ref[...])).astype(BF16)
    if odd:
        ob = _dot(ob_ref[...].astype(BF16), wuv_ref[...])
    else:
        ob = ob_ref[...]
    b = (ob * _silu(zb_ref[...])).astype(BF16)
    y = _dot(a, wo_ref[0:512, :]) + _dot(b, wo_ref[512:1024, :])
    out_ref[...] = x_ref[...] + _rms(y, -1) * gp_ref[...]


def _merge(odd, oa, za, ob, zb, wo, wuv, gp, x2, tm=256):
    n = x2.shape[0]
    tm = min(tm, n)
    row = lambda i: (i, 0)
    const = lambda i: (0, 0)
    return pl.pallas_call(
        functools.partial(_merge_body, odd),
        grid=(n // tm,),
        in_specs=[pl.BlockSpec((tm, 512), row), pl.BlockSpec((tm, 512), row),
                  pl.BlockSpec((tm, ob.shape[1]), row), pl.BlockSpec((tm, 512), row),
                  pl.BlockSpec(wo.shape, const), pl.BlockSpec(wuv.shape, const),
                  pl.BlockSpec((1, D_MODEL), const), pl.BlockSpec((tm, D_MODEL), row)],
        out_specs=pl.BlockSpec((tm, D_MODEL), row),
        out_shape=SDS((n, D_MODEL), F32),
        compiler_params=_params(1),
        name="merge_odd" if odd else "merge_even",
    )(oa, za, ob, zb, wo, wuv, gp, x2)


def _cumsum_body(lf_ref, out_ref):
    t = lf_ref.shape[2]
    carry = jnp.zeros((lf_ref.shape[1], 1), F32)
    for c in range(t // LANES):
        cs = _lane_cumsum(lf_ref[0, :, c * LANES:(c + 1) * LANES]) + carry
        out_ref[0, :, c * LANES:(c + 1) * LANES] = cs
        carry = cs[:, LANES - 1:LANES]


def _cumsum_lanes(lft):
    b, h, t = lft.shape
    spec = pl.BlockSpec((1, h, t), lambda i: (i, 0, 0))
    return pl.pallas_call(_cumsum_body, grid=(b,), in_specs=[spec], out_specs=spec,
                          out_shape=SDS(lft.shape, F32), compiler_params=_params(1), name="fox_cumsum")(lft)


def _causal_blocks(i, tk=TK):
    return (i * TQ + TQ - 1) // tk + 1


def _fox_prompt_body(q_ref, kt_ref, vt_ref, cq_ref, ck_ref, o_ref):
    i = pl.program_id(2)
    r_heads = q_ref.shape[1]
    qpos = i * TQ + _iota((TQ, 1), 0)
    lane = _iota((1, TK), 1)
    qs = [q_ref[0, r] for r in range(r_heads)]
    cqs = [cq_ref[0, 0][:, r:r + 1] for r in range(r_heads)]

    def step(j, carry):
        kt, vt, ck = kt_ref[0, 0, j], vt_ref[0, 0, j], ck_ref[0, 0, j]
        ok = j * TK + lane <= qpos
        out = []
        for r in range(r_heads):
            m, l, acc = carry[r]
            s = jnp.where(ok, _dot(qs[r], kt) + cqs[r] - ck[r:r + 1, :], NEG)
            out.append(_osm_step(s, m, l, acc, vt))
        return tuple(out)

    init = (jnp.full((TQ, 1), NEG, F32), jnp.zeros((TQ, 1), F32), jnp.zeros((TQ, HEAD_DIM), F32))
    res = lax.fori_loop(0, _causal_blocks(i), step, (init,) * r_heads)
    for r in range(r_heads):
        _, l, acc = res[r]
        o_ref[0, r] = acc / l


def _fox_prompt(q, ktb, vtb, cumq, cumk):
    b, h, t, d = q.shape
    g = ktb.shape[1]
    r = h // g
    nk = t // TK
    return pl.pallas_call(
        _fox_prompt_body,
        grid=(b, g, t // TQ),
        in_specs=[pl.BlockSpec((1, r, TQ, d), lambda bi, gi, i: (bi, gi, i, 0)),
                  pl.BlockSpec((1, 1, nk, d, TK), lambda bi, gi, i: (bi, gi, 0, 0, 0)),
                  pl.BlockSpec((1, 1, nk, d, TK), lambda bi, gi, i: (bi, gi, 0, 0, 0)),
                  pl.BlockSpec((1, 1, TQ, r), lambda bi, gi, i: (bi, gi, i, 0)),
                  pl.BlockSpec((1, 1, nk, r, TK), lambda bi, gi, i: (bi, gi, 0, 0, 0))],
        out_specs=pl.BlockSpec((1, r, TQ, d), lambda bi, gi, i: (bi, gi, i, 0)),
        out_shape=SDS((b, h, t, d), F32),
        compiler_params=_params(3),
        name="fox_prompt",
    )(q, ktb, vtb, cumq, cumk)


def _sb_block(z, strict, suf, acc, vt, u):
    sp = _softplus(z)
    ln = jnp.where(strict, -sp, 0.0)
    aft = _dot_split(ln, u) + suf
    a = jnp.where(strict, jnp.exp(z - sp + aft), 0.0)
    acc = acc + _dot_nt(a.astype(BF16), vt)
    return suf + jnp.sum(ln, axis=-1, keepdims=True), acc


def _sb_prompt_body(q_ref, kt_ref, vt_ref, u_ref, o_ref):
    i = pl.program_id(2)
    r_heads = q_ref.shape[1]
    qpos = i * TQ + _iota((TQ, 1), 0)
    lane = _iota((1, TK), 1)
    u = u_ref[...]
    qs = [q_ref[0, r] for r in range(r_heads)]
    nb = _causal_blocks(i)

    def step(jj, carry):
        j = nb - 1 - jj
        kt, vt = kt_ref[0, 0, j], vt_ref[0, 0, j]
        strict = j * TK + lane < qpos
        return tuple(_sb_block(_dot(qs[r], kt), strict, carry[r][0], carry[r][1], vt, u) for r in range(r_heads))

    init = (jnp.zeros((TQ, 1), F32), jnp.zeros((TQ, HEAD_DIM), F32))
    res = lax.fori_loop(0, nb, step, (init,) * r_heads)
    for r in range(r_heads):
        o_ref[0, r] = res[r][1]


def _suffix_matrix(n):
    a = np.arange(n)
    return jnp.asarray((a[:, None] > a[None, :]).astype(np.float32), BF16)


def _sb_prompt(q, ktb, vtb):
    b, h, t, d = q.shape
    g = ktb.shape[1]
    r = h // g
    nk = t // TK
    return pl.pallas_call(
        _sb_prompt_body,
        grid=(b, g, t // TQ),
        in_specs=[pl.BlockSpec((1, r, TQ, d), lambda bi, gi, i: (bi, gi, i, 0)),
                  pl.BlockSpec((1, 1, nk, d, TK), lambda bi, gi, i: (bi, gi, 0, 0, 0)),
                  pl.BlockSpec((1, 1, nk, d, TK), lambda bi, gi, i: (bi, gi, 0, 0, 0)),
                  pl.BlockSpec((TK, TK), lambda bi, gi, i: (0, 0))],
        out_specs=pl.BlockSpec((1, r, TQ, d), lambda bi, gi, i: (bi, gi, i, 0)),
        out_shape=SDS((b, h, t, d), F32),
        compiler_params=_params(3),
        name="sb_prompt",
    )(q, ktb, vtb, _suffix_matrix(TK))


def _mla_prompt_body(ql_ref, qr_ref, ck_ref, kr_ref, o_ref):
    i = pl.program_id(1)
    n_heads = ql_ref.shape[1]
    tk = ck_ref.shape[3]
    qpos = i * TQ + _iota((TQ, 1), 0)
    lane = _iota((1, tk), 1)
    qls = [ql_ref[0, h] for h in range(n_heads)]
    qrs = [qr_ref[0, h] for h in range(n_heads)]

    def step(j, carry):
        ck, kr = ck_ref[0, j], kr_ref[0, j]
        ok = j * tk + lane <= qpos
        out = []
        for h in range(n_heads):
            m, l, acc = carry[h]
            s = jnp.where(ok, (_dot(qls[h], ck) + _dot(qrs[h], kr)) * MLA_SCALE, NEG)
            out.append(_osm_step(s, m, l, acc, ck))
        return tuple(out)

    init = (jnp.full((TQ, 1), NEG, F32), jnp.zeros((TQ, 1), F32), jnp.zeros((TQ, KV_LORA), F32))
    res = lax.fori_loop(0, _causal_blocks(i, tk), step, (init,) * n_heads)
    for h in range(n_heads):
        _, l, acc = res[h]
        o_ref[0, h] = acc / l


def _mla_prompt(ql, qr, ckb, krb):
    b, h, t, c = ql.shape
    nk, tk = ckb.shape[1], ckb.shape[3]
    return pl.pallas_call(
        _mla_prompt_body,
        grid=(b, t // TQ),
        in_specs=[pl.BlockSpec((1, h, TQ, c), lambda bi, i: (bi, 0, i, 0)),
                  pl.BlockSpec((1, h, TQ, ROPE_DIM), lambda bi, i: (bi, 0, i, 0)),
                  pl.BlockSpec((1, nk, c, tk), lambda bi, i: (bi, 0, 0, 0)),
                  pl.BlockSpec((1, nk, ROPE_DIM, tk), lambda bi, i: (bi, 0, 0, 0))],
        out_specs=pl.BlockSpec((1, h, TQ, c), lambda bi, i: (bi, 0, i, 0)),
        out_shape=SDS((b, h, t, c), F32),
        compiler_params=_params(2),
        name="mla_prompt",
    )(ql, qr, ckb, krb)


def _nsa_compressed(q, kc, vct, mcol, qpos, n_real):
    ncp = kc.shape[1]
    lane = _iota((1, ncp), 1)
    c_end = (lane + 1) * CMP_BLOCK - 1
    c_mid = lane.astype(F32) * CMP_BLOCK + 0.5 * (CMP_BLOCK - 1)
    c_ok = (c_end <= qpos) & (lane < n_real)
    sc = _dot(q, kc) - mcol * (qpos.astype(F32) - c_mid)
    sc = jnp.where(c_ok, sc, NEG)
    e = jnp.exp(sc - jnp.max(sc, axis=-1, keepdims=True))
    pc = e / jnp.sum(e, axis=-1, keepdims=True) * c_ok.astype(F32)
    return pc, _dot_nt(pc.astype(BF16), vct)


def _nsa_select(imp2, qpos, ns):
    w = imp2.shape[1]
    lane = _iota((1, w), 1)
    blk = lane // 2
    is_blk = (lane % 2 == 0) & (blk < ns)
    cur = qpos // SEL_BLOCK
    valid = is_blk & (blk <= cur)
    forced = valid & ((blk == 0) | (blk == cur) | (blk == cur - 1))
    score = jnp.where(forced, FORCE_SCORE, jnp.where(valid, imp2, -1.0))
    score = jnp.where(is_blk, score, -jnp.inf)
    cnt = jnp.zeros(score.shape, F32)
    for k in range(ns):
        sk = score[:, 2 * k:2 * k + 1]
        ahead = (sk > score) | ((sk == score) & (2 * k < lane))
        cnt = cnt + ahead.astype(F32)
    return jnp.where(is_blk & (cnt < min(N_SEL, ns)), 1.0, 0.0)


def _sel_expand_matrix(w, first_key, n_keys):
    l = _iota((w, n_keys), 0)
    k = _iota((w, n_keys), 1) + first_key
    return jnp.where(l == 2 * (k // SEL_BLOCK), 1.0, 0.0).astype(BF16)


def _nsa_compress_body(cmp_ref, pe_ref, pool_ref, out_ref, outb_ref):
    pe_mean = jnp.mean(pe_ref[...], axis=1, keepdims=True)
    kvc = _dot_split(cmp_ref[0], pool_ref[...]) + pe_mean
    out_ref[0] = kvc
    outb_ref[0] = kvc.astype(BF16)


def _nsa_compress(kvt, pet, ncp):
    b, _, t = kvt.shape
    a = np.arange(t)[:, None] // CMP_BLOCK == np.arange(ncp)[None, :]
    pool = jnp.asarray(a.astype(np.float32) / CMP_BLOCK, BF16)
    return pl.pallas_call(
        _nsa_compress_body,
        grid=(b,),
        in_specs=[pl.BlockSpec((1, 2 * HEAD_DIM, t), lambda i: (i, 0, 0)),
                  pl.BlockSpec(pet.shape, lambda i: (0, 0)), pl.BlockSpec(pool.shape, lambda i: (0, 0))],
        out_specs=[pl.BlockSpec((1, 2 * HEAD_DIM, ncp), lambda i: (i, 0, 0)),
                   pl.BlockSpec((1, 2 * HEAD_DIM, ncp), lambda i: (i, 0, 0))],
        out_shape=[SDS((b, 2 * HEAD_DIM, ncp), F32), SDS((b, 2 * HEAD_DIM, ncp), BF16)],
        compiler_params=_params(1),
        name="nsa_compress",
    )(kvt, pet, pool)


def _nsa_prompt_body(nc, ns, q_ref, sm_ref, kvc_ref, sel_ref, win_ref, o_ref, mask_ref):
    i = pl.program_id(1)
    nk, tk = sel_ref.shape[1], sel_ref.shape[3]
    d = HEAD_DIM
    qpos = i * TQ + _iota((TQ, 1), 0)
    qf = qpos.astype(F32)
    lane = _iota((1, tk), 1)
    kc = kvc_ref[0, 0:d, :]
    vct = kvc_ref[0, d:2 * d, :]
    ncp = kc.shape[1]
    psum = jnp.zeros((TQ, ncp), F32)
    for h in range(N_HEADS):
        pc, oc = _nsa_compressed(q_ref[0, h], kc, vct, 2.0 ** (-(h + 1)), qpos, nc)
        psum = psum + pc
        o_ref[0, h] = sm_ref[:, 3 * h:3 * h + 1] * oc
    imp2 = psum + pltpu.roll(psum, ncp - 1, axis=1)
    sel = _nsa_select(imp2, qpos, ns).astype(BF16)
    for j in range(nk):
        mask_ref[j] = _dot(sel, _sel_expand_matrix(ncp, j * tk, tk))
    qs = [q_ref[0, h] for h in range(N_HEADS)]
    slopes = [2.0 ** (-(h + 1)) for h in range(N_HEADS)]

    def branch_step(kv_ref, j, dist, ok, carry):
        kt, vt = kv_ref[0, j, 0:d, :], kv_ref[0, j, d:2 * d, :]
        distf = dist.astype(F32)
        out = []
        for h in range(N_HEADS):
            m, l, acc = carry[h]
            s = jnp.where(ok, _dot(qs[h], kt) - slopes[h] * distf, NEG)
            out.append(_osm_step(s, m, l, acc, vt))
        return tuple(out)

    def sel_step(j, carry):
        dist = qpos - (j * tk + lane)
        return branch_step(sel_ref, j, dist, (mask_ref[j] > 0.5) & (dist >= 0), carry)

    def win_step(j, carry):
        dist = qpos - (j * tk + lane)
        return branch_step(win_ref, j, dist, (dist >= 0) & (dist <= WINDOW), carry)

    init = ((jnp.full((TQ, 1), NEG, F32), jnp.zeros((TQ, 1), F32), jnp.zeros((TQ, d), F32)),) * N_HEADS
    nb = _causal_blocks(i, tk)
    lo = jnp.maximum(i * TQ - WINDOW, 0) // tk
    res_s = lax.fori_loop(0, nb, sel_step, init)
    res_w = lax.fori_loop(lo, nb, win_step, init)
    for h in range(N_HEADS):
        g1 = sm_ref[:, 3 * h + 1:3 * h + 2]
        g2 = sm_ref[:, 3 * h + 2:3 * h + 3]
        o_ref[0, h] = o_ref[0, h] + g1 * (res_s[h][2] / res_s[h][1]) + g2 * (res_w[h][2] / res_w[h][1])


def _nsa_prompt(q, small, kvcb, selb, winb, nc, ns):
    b, h, t, d = q.shape
    nk, tk = selb.shape[1], selb.shape[3]
    nq = t // TQ
    ncp = kvcb.shape[2]
    return pl.pallas_call(
        functools.partial(_nsa_prompt_body, nc, ns),
        grid=(b, nq),
        in_specs=[pl.BlockSpec((1, h, TQ, d), lambda bi, i: (bi, 0, i, 0)),
                  pl.BlockSpec((TQ, LANES), lambda bi, i: (bi * nq + i, 0)),
                  pl.BlockSpec((1, 2 * d, ncp), lambda bi, i: (bi, 0, 0)),
                  pl.BlockSpec((1, nk, 2 * d, tk), lambda bi, i: (bi, 0, 0, 0)),
                  pl.BlockSpec((1, nk, 2 * d, tk), lambda bi, i: (bi, 0, 0, 0))],
        out_specs=pl.BlockSpec((1, h, TQ, d), lambda bi, i: (bi, 0, i, 0)),
        out_shape=SDS((b, h, t, d), F32),
        scratch_shapes=[pltpu.VMEM((nk, TQ, tk), F32)],
        compiler_params=_params(2),
        name="nsa_prompt",
    )(q, small, kvcb, selb, winb)


def _own_group_first(o):
    rows, gd = o.shape
    return jnp.where(_iota((rows, 1), 0) < rows // 2, o, pltpu.roll(o, gd // 2, axis=1))


def _fetch(pool, li, pt_ref, buf, sem, seq, slot, start):
    n_pages = buf.shape[1]

    def body(p, c):
        page = pt_ref[seq, p] if start else 0
        cp = pltpu.make_async_copy(pool.at[li, page], buf.at[slot, p], sem.at[slot])
        if start:
            cp.start()
        else:
            cp.wait()
        return c

    lax.fori_loop(0, n_pages, body, 0)


def _pipeline_pages(pools, li, pt_ref, bufs, sems):
    b = pl.program_id(0)
    nb = pl.num_programs(0)
    slot = b % 2

    @pl.when(b == 0)
    def _():
        for pool, buf, sem in zip(pools, bufs, sems):
            _fetch(pool, li, pt_ref, buf, sem, 0, 0, True)

    @pl.when(b + 1 < nb)
    def _():
        for pool, buf, sem in zip(pools, bufs, sems):
            _fetch(pool, li, pt_ref, buf, sem, b + 1, 1 - slot, True)

    for pool, buf, sem in zip(pools, bufs, sems):
        _fetch(pool, li, pt_ref, buf, sem, b, slot, False)
    return slot


def _fox_decode_body(li, past, pt_ref, q_ref, new_ref, lfn_ref, lq_ref, kv_hbm, lf_hbm, o_ref,
                     kvbuf, lfbuf, cum_ref, sem_kv, sem_lf):
    slot = _pipeline_pages((kv_hbm, lf_hbm), li, pt_ref, (kvbuf, lfbuf), (sem_kv, sem_lf))
    n_pages = kvbuf.shape[1]
    gd = FOX_G * HEAD_DIM
    rows = q_ref.shape[1]
    n_tok = rows // N_HEADS
    ch = min(CHUNK, n_pages)

    cs = _lane_cumsum(lfbuf[slot].reshape(n_pages * N_HEADS, PAGE))
    tot = jnp.broadcast_to(cs[:, PAGE - 1:PAGE], cs.shape).reshape(n_pages, N_HEADS, PAGE)
    inc = tot
    sh = 1
    while sh < n_pages:
        inc = inc + jnp.concatenate([jnp.zeros((sh, N_HEADS, PAGE), F32), inc[:n_pages - sh]], axis=0)
        sh *= 2
    cum_ref[...] = cs.reshape(n_pages, N_HEADS, PAGE) + (inc - tot) - inc[n_pages - 1]
    cum_new = _lane_cumsum(lfn_ref[0])
    lane = _iota((1, LANES), 1)
    tok = _iota((rows, 1), 0) % n_tok
    q = q_ref[0]
    cq = jnp.sum(jnp.where(lane <= tok, lq_ref[0], 0.0), axis=-1, keepdims=True)

    def step(c, carry):
        m, l, acc = carry
        ss, vts = [], []
        for k in range(ch):
            p = c * ch + k
            ss.append(_dot(q, kvbuf[slot, p, 0:gd, :].astype(BF16)) + cq - _rep_rows(cum_ref[p], n_tok))
            vts.append(kvbuf[slot, p, gd:2 * gd, :].astype(BF16))
        return _osm_chunk(ss, m, l, acc, vts)

    init = (jnp.full((rows, 1), NEG, F32), jnp.zeros((rows, 1), F32), jnp.zeros((rows, gd), F32))
    m, l, acc = lax.fori_loop(0, n_pages // ch, step, init)
    s = _dot(q, new_ref[0, 0:gd, :].astype(BF16)) + cq - _rep_rows(cum_new, n_tok)
    m, l, acc = _osm_step(jnp.where(lane <= tok, s, NEG), m, l, acc, new_ref[0, gd:2 * gd, :].astype(BF16))
    o_ref[0] = _own_group_first(acc / l)


def _decode_call(body, name, pt, ins, in_specs, pools, out_block, out_shape, scratch):
    nb = pt.shape[0]
    any_spec = pl.BlockSpec(memory_space=pl.ANY)
    grid_spec = pltpu.PrefetchScalarGridSpec(
        num_scalar_prefetch=1, grid=(nb,),
        in_specs=list(in_specs) + [any_spec] * len(pools),
        out_specs=pl.BlockSpec(out_block, lambda b, pt_ref: (b, 0, 0)),
        scratch_shapes=scratch)
    return pl.pallas_call(body, grid_spec=grid_spec, out_shape=out_shape,
                          compiler_params=_params(1), name=name)(pt, *ins, *pools)


def _seq_spec(shape):
    return pl.BlockSpec((1,) + tuple(shape[1:]), lambda b, pt_ref: (b,) + (0,) * (len(shape) - 1))


def _fox_decode(li, pt, q, newt, lfnew, lq, kv_pool, lf_pool):
    s, rows, d = q.shape
    n_pages = pt.shape[1]
    past = n_pages * PAGE
    scratch = [pltpu.VMEM((2, n_pages, 2 * d, PAGE), F32), pltpu.VMEM((2, n_pages, N_HEADS, PAGE), F32),
               pltpu.VMEM((n_pages, N_HEADS, PAGE), F32),
               pltpu.SemaphoreType.DMA((2,)), pltpu.SemaphoreType.DMA((2,))]
    ins = (q, newt, lfnew, lq)
    return _decode_call(functools.partial(_fox_decode_body, li, past), "fox_decode", pt, ins,
                        [_seq_spec(a.shape) for a in ins], (kv_pool, lf_pool),
                        (1, rows, d), SDS((s, rows, d), F32), scratch)


def _sb_decode_body(li, past, pt_ref, q_ref, new_ref, u_ref, kv_hbm, o_ref, kvbuf, sem_kv):
    slot = _pipeline_pages((kv_hbm,), li, pt_ref, (kvbuf,), (sem_kv,))
    n_pages = kvbuf.shape[1]
    gd = SB_G * HEAD_DIM
    rows = q_ref.shape[1]
    n_tok = rows // N_HEADS
    ch = min(CHUNK, n_pages)
    n_chunks = n_pages // ch
    lane = _iota((1, LANES), 1)
    tok = _iota((rows, 1), 0) % n_tok
    u = u_ref[...]
    q = q_ref[0]
    carry = _sb_block(_dot(q, new_ref[0, 0:gd, :].astype(BF16)), lane < tok, jnp.zeros((rows, 1), F32),
                      jnp.zeros((rows, gd), F32), new_ref[0, gd:2 * gd, :].astype(BF16), u)

    def step(cc, carry):
        suf, acc = carry
        c = n_chunks - 1 - cc
        zs, sps, vts, sums = [], [], [], []
        for k in range(ch):
            p = c * ch + k
            z = _dot(q, kvbuf[slot, p, 0:gd, :].astype(BF16))
            sp = _softplus(z)
            zs.append(z)
            sps.append(sp)
            vts.append(kvbuf[slot, p, gd:2 * gd, :].astype(BF16))
            sums.append(jnp.sum(sp, axis=-1, keepdims=True))
        for k in reversed(range(ch)):
            aft = suf - _dot_split(sps[k], u)
            acc = acc + _dot_nt(jnp.exp(zs[k] - sps[k] + aft).astype(BF16), vts[k])
            suf = suf - sums[k]
        return suf, acc

    _, acc = lax.fori_loop(0, n_chunks, step, carry)
    o_ref[0] = _own_group_first(acc)


def _sb_decode(li, pt, q, newt, kv_pool):
    s, rows, d = q.shape
    n_pages = pt.shape[1]
    scratch = [pltpu.VMEM((2, n_pages, 2 * d, PAGE), F32), pltpu.SemaphoreType.DMA((2,))]
    u = _suffix_matrix(PAGE)
    in_specs = [_seq_spec(q.shape), _seq_spec(newt.shape), pl.BlockSpec((PAGE, PAGE), lambda b, pt_ref: (0, 0))]
    return _decode_call(functools.partial(_sb_decode_body, li, n_pages * PAGE), "sb_decode", pt, (q, newt, u),
                        in_specs, (kv_pool,), (1, rows, d), SDS((s, rows, d), F32), scratch)


def _mla_decode_body(li, past, pt_ref, ql_ref, qr_ref, new_ref, lat_hbm, o_ref, buf, sem):
    slot = _pipeline_pages((lat_hbm,), li, pt_ref, (buf,), (sem,))
    n_pages = buf.shape[1]
    rows = ql_ref.shape[1]
    n_tok = rows // N_HEADS
    lane = _iota((1, LANES), 1)
    tok = _iota((rows, 1), 0) % n_tok
    ql = ql_ref[0]
    qr = qr_ref[0]

    ch = min(CHUNK, n_pages)

    def step(c, carry):
        m, l, acc = carry
        ss, cks = [], []
        for k in range(ch):
            p = c * ch + k
            ck = buf[slot, p, 0:KV_LORA, :].astype(BF16)
            kr = buf[slot, p, KV_LORA:LAT, :].astype(BF16)
            ss.append((_dot(ql, ck) + _dot(qr, kr)) * MLA_SCALE)
            cks.append(ck)
        return _osm_chunk(ss, m, l, acc, cks)

    init = (jnp.full((rows, 1), NEG, F32), jnp.zeros((rows, 1), F32), jnp.zeros((rows, KV_LORA), F32))
    m, l, acc = lax.fori_loop(0, n_pages // ch, step, init)
    ck = new_ref[0, 0:KV_LORA, :].astype(BF16)
    kr = new_ref[0, KV_LORA:LAT, :].astype(BF16)
    s = (_dot(ql, ck) + _dot(qr, kr)) * MLA_SCALE
    m, l, acc = _osm_step(jnp.where(lane <= tok, s, NEG), m, l, acc, ck)
    o_ref[0] = acc / l


def _mla_decode(li, pt, ql, qr, newt, lat_pool):
    s, rows, c = ql.shape
    n_pages = pt.shape[1]
    scratch = [pltpu.VMEM((2, n_pages, LAT, PAGE), F32), pltpu.SemaphoreType.DMA((2,))]
    ins = (ql, qr, newt)
    return _decode_call(functools.partial(_mla_decode_body, li, n_pages * PAGE), "mla_decode", pt, ins,
                        [_seq_spec(a.shape) for a in ins], (lat_pool,), (1, rows, c), SDS((s, rows, c), F32), scratch)


def _nsa_decode_body(li, past, ncp, pt_ref, q_ref, gate_ref, new_ref, pe_ref, win_ref, pool_ref, kv_hbm, o_ref,
                     kvbuf, pm_ref, sem_kv):
    slot = _pipeline_pages((kv_hbm,), li, pt_ref, (kvbuf,), (sem_kv,))
    n_pages = kvbuf.shape[1]
    d = HEAD_DIM
    rows = q_ref.shape[1]
    n_tok = rows // N_HEADS
    per_page = PAGE // CMP_BLOCK
    nc = n_pages * per_page
    ns = (past + n_tok + SEL_BLOCK - 1) // SEL_BLOCK
    q = q_ref[0]
    lane = _iota((1, LANES), 1)
    tok = _iota((rows, 1), 0) % n_tok
    qpos = past + tok
    qf = qpos.astype(F32)
    mcol = _head_slopes_col(rows, n_tok)

    per_tile = LANES // per_page
    tiles = []
    ch = min(CHUNK, n_pages)
    for t0 in range(0, n_pages, per_tile):
        def cmp_step(c, acc, t0=t0):
            parts = [_dot(kvbuf[slot, t0 + c * ch + k, 0:2 * d, :].astype(BF16), pool_ref[c * ch + k])
                     for k in range(ch)]
            return acc + functools.reduce(jnp.add, parts)

        tiles.append(lax.fori_loop(0, min(per_tile, n_pages - t0) // ch, cmp_step, jnp.zeros((2 * d, LANES), F32)))
    kvc = tiles[0] if len(tiles) == 1 else jnp.concatenate(tiles, axis=1)
    kvc = (kvc + jnp.mean(pe_ref[...], axis=1, keepdims=True)).astype(BF16)
    pc, oc = _nsa_compressed(q, kvc[0:d], kvc[d:2 * d], mcol, qpos, nc)
    psum = jnp.sum(pc.reshape(N_HEADS, n_tok, ncp), axis=0)
    imp2 = psum + pltpu.roll(psum, ncp - 1, axis=1)
    imp2 = jnp.concatenate([imp2, jnp.zeros((n_tok, LANES), F32)], axis=1)
    sel = _nsa_select(imp2, past + _iota((n_tok, 1), 0), ns)
    per_sel = PAGE // SEL_BLOCK
    for p in range(n_pages + 1):
        pm = sel[:, 2 * per_sel * p:2 * per_sel * p + 1]
        for b2 in range(1, per_sel):
            pm = jnp.where(lane < b2 * SEL_BLOCK, pm, sel[:, 2 * (per_sel * p + b2):2 * (per_sel * p + b2) + 1])
        pm_ref[p] = jnp.broadcast_to(pm, (n_tok, LANES))

    def sel_step(c, carry):
        m, l, acc = carry
        ss, vts = [], []
        for k in range(ch):
            p = c * ch + k
            kpos = p * PAGE + lane
            s = _dot(q, kvbuf[slot, p, 2 * d:3 * d, :].astype(BF16)) - mcol * (qf - kpos.astype(F32))
            ss.append(jnp.where(_tile_rows(pm_ref[p], N_HEADS) > 0.5, s, NEG))
            vts.append(kvbuf[slot, p, 3 * d:4 * d, :].astype(BF16))
        return _osm_chunk(ss, m, l, acc, vts)

    kpos = past + lane
    s = _dot(q, new_ref[0, 2 * d:3 * d, :].astype(BF16)) - mcol * (qf - kpos.astype(F32))
    s = jnp.where((_tile_rows(pm_ref[n_pages], N_HEADS) > 0.5) & (kpos <= qpos), s, NEG)
    init = (jnp.full((rows, 1), NEG, F32), jnp.zeros((rows, 1), F32), jnp.zeros((rows, d), F32))
    carry = _osm_step(s, init[0], init[1], init[2], new_ref[0, 3 * d:4 * d, :].astype(BF16))
    m, l, acc = lax.fori_loop(0, n_pages // ch, sel_step, carry)
    o_sel = acc / l

    wb = win_ref.shape[3]
    wlane = _iota((1, wb), 1)
    dist = qpos - (past - wb + wlane)
    sw = _dot(q, win_ref[0, 0, 0:d, :].astype(BF16)) - mcol * dist.astype(F32)
    sw = jnp.where((dist >= 0) & (dist <= WINDOW), sw, NEG)
    dist_n = qpos - kpos
    sn = _dot(q, new_ref[0, 4 * d:5 * d, :].astype(BF16)) - mcol * dist_n.astype(F32)
    sn = jnp.where((dist_n >= 0) & (dist_n <= WINDOW), sn, NEG)
    mx = jnp.maximum(jnp.max(sw, axis=-1, keepdims=True), jnp.max(sn, axis=-1, keepdims=True))
    pw = jnp.exp(sw - mx)
    pn = jnp.exp(sn - mx)
    lw = jnp.sum(pw, axis=-1, keepdims=True) + jnp.sum(pn, axis=-1, keepdims=True)
    o_win = (_dot_nt(pw.astype(BF16), win_ref[0, 0, d:2 * d, :].astype(BF16))
             + _dot_nt(pn.astype(BF16), new_ref[0, 5 * d:6 * d, :].astype(BF16))) / lw
    g = gate_ref[0]
    o_ref[0] = g[:, 0:1] * oc + g[:, 1:2] * o_sel + g[:, 2:3] * o_win


def _nsa_decode(li, pt, q, gates, newt, pet, wint, kv_pool):
    s, rows, d = q.shape
    n_pages = pt.shape[1]
    past = n_pages * PAGE
    ncp = -(-(n_pages * (PAGE // CMP_BLOCK)) // LANES) * LANES
    wb = wint.shape[3]
    n_tok = rows // N_HEADS
    per_page = PAGE // CMP_BLOCK
    per_tile = LANES // per_page
    k_i, key, col = np.meshgrid(np.arange(per_tile), np.arange(PAGE), np.arange(LANES), indexing='ij')
    pool = jnp.asarray((col == k_i * per_page + key // CMP_BLOCK).astype(np.float32) / CMP_BLOCK, BF16)
    scratch = [pltpu.VMEM((2, n_pages, 4 * d, PAGE), F32), pltpu.VMEM((n_pages + 1, n_tok, LANES), F32),
               pltpu.SemaphoreType.DMA((2,))]
    in_specs = [_seq_spec(q.shape), _seq_spec(gates.shape), _seq_spec(newt.shape),
                pl.BlockSpec(pet.shape, lambda b, pt_ref: (0, 0)),
                pl.BlockSpec((1, 1, 2 * d, wb), lambda b, pt_ref: (li, b, 0, 0)),
                pl.BlockSpec(pool.shape, lambda b, pt_ref: (0, 0, 0))]
    return _decode_call(functools.partial(_nsa_decode_body, li, past, ncp), "nsa_decode", pt,
                        (q, gates, newt, pet, wint, pool), in_specs, (kv_pool,),
                        (1, rows, d), SDS((s, rows, d), F32), scratch)


def _pad_lanes(a, width=LANES):
    return jnp.pad(a, [(0, 0)] * (a.ndim - 1) + [(0, width - a.shape[-1])])


def _heads_first(a, b, t, h):
    return a.reshape(b, t, h, -1).transpose(0, 2, 1, 3)


def _rows_from_heads(o):
    b, h, t, w = o.shape
    return o.transpose(0, 2, 1, 3).reshape(b * t, h * w)


def _key_blocks(kvt, lo, hi, tk=TK):
    b, _, t = kvt.shape
    return kvt[:, lo:hi].reshape(b, hi - lo, t // tk, tk).transpose(0, 2, 1, 3)


def _group_key_blocks(kvt, lo, g):
    b, _, t = kvt.shape
    d = HEAD_DIM
    return kvt[:, lo:lo + g * d].reshape(b, g, d, t // TK, TK).transpose(0, 1, 3, 2, 4)


def _sample_rows(a, s, t, h):
    return a.reshape(s, t, h, -1).transpose(0, 2, 1, 3).reshape(s, h * t, -1)


def _sample_rows_back(o, t):
    s, rows, w = o.shape
    h = rows // t
    return o.reshape(s, h, t, w).transpose(0, 2, 1, 3).reshape(s * t, h * w)


def _group_slots(q, g):
    s, rows, d = q.shape
    rg = rows // g
    parts = [jnp.pad(q[:, k * rg:(k + 1) * rg], ((0, 0), (0, 0), (k * d, (g - 1 - k) * d))) for k in range(g)]
    return jnp.concatenate(parts, axis=1)


def _own_group(o, g):
    assert g == 2
    return o[:, :, :o.shape[2] // g]


def _new_pages(kvt, s, t):
    c = kvt.shape[1]
    return _pad_lanes(kvt[0].reshape(c, s, t).transpose(1, 0, 2))


def _even_weights(w_in, b_f):
    o = np.cumsum((0, 512, 384, 24, 512, 512, 256, 8, 512))
    small = _pad_lanes(jnp.concatenate([w_in[:, o[2]:o[3]], w_in[:, o[6]:o[7]]], axis=1))
    wn = jnp.concatenate([w_in[:, o[0]:o[1]], w_in[:, o[3]:o[4]], w_in[:, o[4]:o[5]], w_in[:, o[7]:o[8]], small],
                         axis=1).astype(BF16)
    wt = jnp.concatenate([w_in[:, o[1]:o[2]], w_in[:, o[5]:o[6]]], axis=1).T.astype(BF16)
    bs = jnp.zeros((1, LANES), F32).at[0, 3 * N_HEADS:3 * N_HEADS + N_HEADS].set(b_f)
    return wn, wt, bs


def _odd_weights(w_in, w_uq, w_uk, w_uv):
    o = np.cumsum((0, 512, 256, 512, 256, 128, 32, 512))
    wn = jnp.concatenate([w_in[:, o[0]:o[1]], w_in[:, o[2]:o[3]], w_in[:, o[6]:o[7]], w_in[:, o[3]:o[4]]],
                         axis=1).astype(BF16)
    wt = jnp.concatenate([w_in[:, o[1]:o[2]], w_in[:, o[4]:o[5]], w_in[:, o[5]:o[6]]], axis=1).T.astype(BF16)
    uq = w_uq.reshape(Q_LORA, N_HEADS, NOPE_DIM + ROPE_DIM)
    wuqn = uq[:, :, :NOPE_DIM].reshape(Q_LORA, N_HEADS * NOPE_DIM).astype(BF16)
    half = ROPE_DIM // 2
    wuqr = uq[:, :, NOPE_DIM:].reshape(Q_LORA, N_HEADS, 2, half).transpose(0, 2, 1, 3).reshape(Q_LORA, 2 * LANES)
    eye = jnp.eye(N_HEADS, dtype=F32)
    wuk = jnp.einsum('chn,hg->hngc', w_uk, eye).reshape(N_HEADS * NOPE_DIM, N_HEADS * KV_LORA).astype(BF16)
    wuv = jnp.einsum('chv,hg->hcgv', w_uv, eye).reshape(N_HEADS * KV_LORA, N_HEADS * V_DIM).astype(BF16)
    return wn, wt, wuqn, wuqr.astype(BF16), wuk, wuv


def _rope_tables(pos):
    half = ROPE_DIM // 2
    inv = ROPE_BASE ** (-jnp.arange(half, dtype=F32) / half)
    ang = pos.astype(F32)[:, None] * inv[None, :]
    cos, sin = jnp.cos(ang), jnp.sin(ang)
    rep = LANES // half
    return jnp.tile(cos, (1, rep)), jnp.tile(sin, (1, rep)), cos.T, sin.T


def _even_prompt(x2, b, g_pre, g_post, wts, pet, w_out):
    wn, wt, bs = wts
    t = x2.shape[0] // b
    h, d = N_HEADS, HEAD_DIM
    qn, zn, qf, zf, small, kvt, kvtb = _proj_even(x2, g_pre, wn, wt, bs, b)
    nc = t // CMP_BLOCK
    ncp = -(-nc // LANES) * LANES
    _, kvcb = _nsa_compress(kvt, pet, ncp)
    o_n = _nsa_prompt(_heads_first(qn, b, t, h), small, kvcb, _key_blocks(kvtb, 2 * d, 4 * d, TK_NSA),
                      _key_blocks(kvtb, 4 * d, 6 * d, TK_NSA), nc, t // SEL_BLOCK)
    logf = small[:, 3 * h:4 * h].reshape(b, t, h)
    cum = _cumsum_lanes(logf.transpose(0, 2, 1))
    r = h // FOX_G
    cumq = cum.reshape(b, FOX_G, r, t).transpose(0, 1, 3, 2)
    cumk = cum.reshape(b, FOX_G, r, t // TK, TK).transpose(0, 1, 3, 2, 4)
    o_f = _fox_prompt(_heads_first(qf, b, t, h), _group_key_blocks(kvtb, 6 * d, FOX_G),
                      _group_key_blocks(kvtb, 8 * d, FOX_G), cumq, cumk)
    x_new = _merge(False, _rows_from_heads(o_n), zn, _rows_from_heads(o_f), zf, w_out, w_out[:8, :128], g_post, x2)
    kv_rows = kvt.transpose(0, 2, 1)
    n_win = min(WINDOW, t)
    outs = (kv_rows[:, :, 0:4 * d].reshape(b, t, 4, 1, d),
            kv_rows[:, t - n_win:, 4 * d:6 * d].reshape(b, n_win, 2, 1, d),
            kv_rows[:, :, 6 * d:10 * d].reshape(b, t, 2, FOX_G, d), logf)
    return x_new, outs


def _even_sample(x2, s, li, pt, g_pre, g_post, wts, pet, w_out, nsa_pool, win_state, win_t, fox_pool, lf_pool):
    wn, wt, bs = wts
    t = x2.shape[0] // s
    h, d = N_HEADS, HEAD_DIM
    qn, zn, qf, zf, small, kvt, _ = _proj_even(x2, g_pre, wn, wt, bs, 1)
    gates = _sample_rows(small[:, 0:3 * h], s, t, h)
    newt = _new_pages(kvt, s, t)
    o_n = _nsa_decode(li, pt, _sample_rows(qn, s, t, h), gates, newt[:, 0:6 * d], pet, win_t, nsa_pool)
    logf = small[:, 3 * h:4 * h].reshape(s, t, h)
    lfnew = _pad_lanes(logf.transpose(0, 2, 1))
    lq = _pad_lanes(jnp.broadcast_to(logf.transpose(0, 2, 1)[:, :, None, :], (s, h, t, t)).reshape(s, h * t, t))
    o_f = _fox_decode(li, pt, _group_slots(_sample_rows(qf, s, t, h), FOX_G), newt[:, 6 * d:10 * d], lfnew, lq,
                      fox_pool, lf_pool)
    x_new = _merge(False, _sample_rows_back(o_n, t), zn, _sample_rows_back(_own_group(o_f, FOX_G), t), zf, w_out,
                   w_out[:8, :128],
                   g_post, x2)
    kv_rows = kvt[0].T.reshape(s, t, 10 * d)
    new_win = kv_rows[:, :, 4 * d:6 * d].reshape(s, t, 2, 1, d)
    outs = (kv_rows[:, :, 0:4 * d].reshape(s, t, 4, 1, d),
            jnp.concatenate([win_state[li], new_win], axis=1)[:, t:],
            kv_rows[:, :, 6 * d:10 * d].reshape(s, t, 2, FOX_G, d), logf)
    return x_new, outs


def _odd_prompt(x2, b, g_pre, g_post, wts, qn, kvn, w_out):
    wn, wt, wuqn, wuqr, wuk, wuv = wts
    t = x2.shape[0] // b
    h, d = N_HEADS, HEAD_DIM
    cosq, sinq, cost, sint = _rope_tables(jnp.arange(t, dtype=jnp.int32))
    qs, zs, zm, ql, qr, kvt, kvtb, lat, latb = _proj_odd(x2, g_pre, wn, wt, qn, kvn, wuqn, wuqr, wuk,
                                                         cosq, sinq, cost, sint, b)
    o_s = _sb_prompt(_heads_first(qs, b, t, h), _group_key_blocks(kvtb, 0, SB_G), _group_key_blocks(kvtb, 2 * d, SB_G))
    half = ROPE_DIM // 2
    qr_h = qr.reshape(b, t, 2, h, half).transpose(0, 3, 1, 2, 4).reshape(b, h, t, ROPE_DIM)
    o_l = _mla_prompt(_heads_first(ql, b, t, h), qr_h, _key_blocks(latb, 0, KV_LORA, TK_MLA),
                      _key_blocks(latb, KV_LORA, LAT, TK_MLA))
    x_new = _merge(True, _rows_from_heads(o_s), zs, _rows_from_heads(o_l), zm, w_out, wuv, g_post, x2)
    outs = (kvt.transpose(0, 2, 1).reshape(b, t, 2, SB_G, d), lat.transpose(0, 2, 1))
    return x_new, outs


def _odd_sample(x2, s, li, pt, g_pre, g_post, wts, qn, kvn, w_out, sb_pool, lat_pool):
    wn, wt, wuqn, wuqr, wuk, wuv = wts
    t = x2.shape[0] // s
    h, d = N_HEADS, HEAD_DIM
    past = pt.shape[1] * PAGE
    pos = jnp.tile(past + jnp.arange(t, dtype=jnp.int32), s)
    cosq, sinq, cost, sint = _rope_tables(pos)
    qs, zs, zm, ql, qr, kvt, _, lat, _ = _proj_odd(x2, g_pre, wn, wt, qn, kvn, wuqn, wuqr, wuk,
                                                   cosq, sinq, cost, sint, 1)
    o_s = _own_group(_sb_decode(li, pt, _group_slots(_sample_rows(qs, s, t, h), SB_G), _new_pages(kvt, s, t),
                                sb_pool), SB_G)
    half = ROPE_DIM // 2
    qr_h = qr.reshape(s, t, 2, h, half).transpose(0, 3, 1, 2, 4).reshape(s, h * t, ROPE_DIM)
    o_l = _mla_decode(li, pt, _sample_rows(ql, s, t, h), qr_h, _new_pages(lat, s, t), lat_pool)
    x_new = _merge(True, _sample_rows_back(o_s, t), zs, _sample_rows_back(o_l, t), zm, w_out, wuv, g_post, x2)
    outs = (kvt[0].T.reshape(s, t, 2, SB_G, d), lat[0].T.reshape(s, t, LAT))
    return x_new, outs


def _feature_major_pool(pool):
    nd = pool.ndim
    p = pool.transpose((0, 1) + tuple(range(3, nd)) + (2,))
    return p.reshape(p.shape[0], p.shape[1], -1, p.shape[-1])


def kernel(x_prompt, x_sample, cache_nsa_kv, state_nsa_win, cache_fox_kv, cache_fox_logf, cache_sb_kv, cache_mla_latent, page_table, norm_pre, norm_post, w_in_e, b_f, nsa_pe, w_out_e, w_in_o, mla_q_norm, mla_kv_norm, w_uq, w_uk, w_uv, w_out_o):
    b, t_p, _ = x_prompt.shape
    s, t_s, _ = x_sample.shape
    depth = norm_pre.shape[0]
    xp = x_prompt.reshape(b * t_p, D_MODEL)
    xs = x_sample.reshape(s * t_s, D_MODEL)
    nsa_pool = _feature_major_pool(cache_nsa_kv)
    fox_pool = _feature_major_pool(cache_fox_kv)
    lf_pool = _feature_major_pool(cache_fox_logf)
    sb_pool = _feature_major_pool(cache_sb_kv)
    lat_pool = _feature_major_pool(cache_mla_latent)
    win_t = _feature_major_pool(state_nsa_win)
    pt = page_table.astype(jnp.int32)
    outs_p = [[] for _ in range(6)]
    outs_s = [[] for _ in range(6)]
    for layer in range(depth):
        i = layer // 2
        g_pre = norm_pre[layer][None, :]
        g_post = norm_post[layer][None, :]
        if layer % 2 == 0:
            wts = _even_weights(w_in_e[i], b_f[i])
            pet = nsa_pe[i].transpose(0, 2, 1).reshape(2 * HEAD_DIM, CMP_BLOCK)
            w_out = w_out_e[i].astype(BF16)
            xp, o_p = _even_prompt(xp, b, g_pre, g_post, wts, pet, w_out)
            xs, o_s = _even_sample(xs, s, i, pt, g_pre, g_post, wts, pet, w_out, nsa_pool, state_nsa_win, win_t,
                                   fox_pool, lf_pool)
            for k in range(4):
                outs_p[k].append(o_p[k])
                outs_s[k].append(o_s[k])
        else:
            wts = _odd_weights(w_in_o[i], w_uq[i], w_uk[i], w_uv[i])
            qn = mla_q_norm[i][None, :]
            kvn = mla_kv_norm[i][:, None]
            w_out = w_out_o[i].astype(BF16)
            xp, o_p = _odd_prompt(xp, b, g_pre, g_post, wts, qn, kvn, w_out)
            xs, o_s = _odd_sample(xs, s, i, pt, g_pre, g_post, wts, qn, kvn, w_out, sb_pool, lat_pool)
            for k in range(2):
                outs_p[4 + k].append(o_p[k])
                outs_s[4 + k].append(o_s[k])
    res = [xp.reshape(b, t_p, D_MODEL), xs.reshape(s, t_s, D_MODEL)]
    for k in range(6):
        res.append(jnp.stack(outs_p[k]))
        res.append(jnp.stack(outs_s[k]))
    return tuple(res)
```

```python
import functools
import math

import numpy as np
import jax
import jax.numpy as jnp
from jax import lax
from jax.experimental import pallas as pl
from jax.experimental.pallas import tpu as pltpu

F32 = jnp.float32
BF16 = jnp.bfloat16
SDS = jax.ShapeDtypeStruct

D_MODEL = 1024
HEAD_DIM = 64
N_HEADS = 8
PAGE = 128
EPS = 1e-6
NEG = -1e30
CMP_BLOCK = 32
SEL_BLOCK = 64
N_SEL = 16
WINDOW = 512
FORCE_SCORE = 1e4
FOX_G = 2
SB_G = 2
Q_LORA = 256
KV_LORA = 128
NOPE_DIM = 64
ROPE_DIM = 32
V_DIM = 64
ROPE_BASE = 10000.0
MLA_SCALE = 1.0 / math.sqrt(NOPE_DIM + ROPE_DIM)
LAT = KV_LORA + ROPE_DIM
QSCALE = 1.0 / math.sqrt(HEAD_DIM)

LANES = 128
TQ = 128
TK = 512
TK_NSA = 256
TK_MLA = 128
CHUNK = 64
CHUNK_NSA = 32
VMEM_LIMIT = 56 * 1024 * 1024


def _params(n_axes, vmem=VMEM_LIMIT):
    return pltpu.CompilerParams(dimension_semantics=("arbitrary",) * n_axes, vmem_limit_bytes=vmem)


def _dot(a, b):
    return jnp.dot(a, b, preferred_element_type=F32)


def _dot_nt(a, b):
    return lax.dot_general(a, b, (((1,), (1,)), ((), ())), preferred_element_type=F32)


def _dot_split(x, w):
    hi = x.astype(BF16)
    lo = (x - hi.astype(F32)).astype(BF16)
    return _dot(hi, w) + _dot(lo, w)


def _rms(x, axis):
    return x * lax.rsqrt(jnp.mean(x * x, axis=axis, keepdims=True) + EPS)


def _sigmoid(x):
    return 1.0 / (1.0 + jnp.exp(-x))


def _softplus(x):
    return jnp.maximum(x, 0.0) + jnp.log1p(jnp.exp(-jnp.abs(x)))


def _silu(x):
    return x * _sigmoid(x)


def _iota(shape, axis):
    return lax.broadcasted_iota(jnp.int32, shape, axis)


def _lane_cumsum(x):
    lane = _iota(x.shape, 1)
    sh = 1
    while sh < x.shape[1]:
        x = x + jnp.where(lane >= sh, pltpu.roll(x, sh, axis=1), 0.0)
        sh *= 2
    return x


def _osm_step(s, m, l, acc, vt):
    m2 = jnp.maximum(m, jnp.max(s, axis=-1, keepdims=True))
    p = jnp.exp(s - m2)
    a = jnp.exp(m - m2)
    l2 = a * l + jnp.sum(p, axis=-1, keepdims=True)
    acc2 = a * acc + _dot_nt(p.astype(BF16), vt)
    return m2, l2, acc2


def _osm_chunk(s_list, m, l, acc, vt_list):
    mx = functools.reduce(jnp.maximum, s_list)
    m2 = jnp.maximum(m, jnp.max(mx, axis=-1, keepdims=True))
    a = jnp.exp(m - m2)
    ps = [jnp.exp(s - m2) for s in s_list]
    l2 = a * l + jnp.sum(functools.reduce(jnp.add, ps), axis=-1, keepdims=True)
    pv = functools.reduce(jnp.add, [_dot_nt(p.astype(BF16), vt) for p, vt in zip(ps, vt_list)])
    return m2, l2, a * acc + pv


def _rep_rows(x, n):
    return jnp.concatenate([jnp.broadcast_to(x[r:r + 1], (n, x.shape[1])) for r in range(x.shape[0])], axis=0)


def _tile_rows(x, n):
    return jnp.concatenate([x] * n, axis=0)


def _head_slopes_col(rows, per_head):
    h = _iota((rows, 1), 0) // per_head
    m = jnp.zeros((rows, 1), F32)
    for hh in range(N_HEADS):
        m = jnp.where(h == hh, 2.0 ** (-(hh + 1)), m)
    return m


def _proj_even_body(x_ref, g_ref, wn_ref, wt_ref, bs_ref,
                    qn_ref, zn_ref, qf_ref, zf_ref, sm_ref, kvt_ref, kvtb_ref):
    xb = (_rms(x_ref[...], -1) * g_ref[...]).astype(BF16)
    qn_ref[...] = (_dot(xb, wn_ref[:, 0:512]) * QSCALE).astype(BF16)
    zn_ref[...] = _dot(xb, wn_ref[:, 512:1024])
    qf_ref[...] = (_dot(xb, wn_ref[:, 1024:1536]) * QSCALE).astype(BF16)
    zf_ref[...] = _dot(xb, wn_ref[:, 1536:2048])
    s = _dot(xb, wn_ref[:, 2048:2176]) + bs_ref[...]
    lane = _iota(s.shape, 1)
    sm_ref[...] = jnp.where(lane < 3 * N_HEADS, _sigmoid(s), -_softplus(-s))
    ht = _dot_nt(wt_ref[...], xb)
    kvt_ref[0] = ht
    kvtb_ref[0] = ht.astype(BF16)


def _proj_even(x2, g, wn, wt, bs, nb, tm=256):
    n = x2.shape[0]
    tm = min(tm, n)
    t = n // nb
    nt = t // tm
    row = lambda i: (i, 0)
    const = lambda i: (0, 0)
    tr = lambda i: (i // nt, 0, i % nt)
    ct = wt.shape[0]
    return pl.pallas_call(
        _proj_even_body,
        grid=(n // tm,),
        in_specs=[pl.BlockSpec((tm, D_MODEL), row), pl.BlockSpec((1, D_MODEL), const),
                  pl.BlockSpec(wn.shape, const), pl.BlockSpec(wt.shape, const), pl.BlockSpec((1, LANES), const)],
        out_specs=[pl.BlockSpec((tm, 512), row), pl.BlockSpec((tm, 512), row), pl.BlockSpec((tm, 512), row),
                   pl.BlockSpec((tm, 512), row), pl.BlockSpec((tm, LANES), row),
                   pl.BlockSpec((1, ct, tm), tr), pl.BlockSpec((1, ct, tm), tr)],
        out_shape=[SDS((n, 512), BF16), SDS((n, 512), F32), SDS((n, 512), BF16), SDS((n, 512), F32),
                   SDS((n, LANES), F32), SDS((nb, ct, t), F32), SDS((nb, ct, t), BF16)],
        compiler_params=_params(1),
        name="proj_even",
    )(x2, g, wn, wt, bs)


def _proj_odd_body(x_ref, g_ref, wn_ref, wt_ref, qn_ref, kvn_ref, wuqn_ref, wuqr_ref, wuk_ref,
                   cq_ref, sq_ref, ct_ref, st_ref,
                   qs_ref, zs_ref, zm_ref, ql_ref, qr_ref, kvt_ref, kvtb_ref, lt_ref, ltb_ref):
    xb = (_rms(x_ref[...], -1) * g_ref[...]).astype(BF16)
    qs_ref[...] = (_dot(xb, wn_ref[:, 0:512]) * QSCALE).astype(BF16)
    zs_ref[...] = _dot(xb, wn_ref[:, 512:1024])
    zm_ref[...] = _dot(xb, wn_ref[:, 1024:1536])
    cq = _dot(xb, wn_ref[:, 1536:1792])
    cb = (_rms(cq, -1) * qn_ref[...]).astype(BF16)
    nope = _dot(cb, wuqn_ref[...]).astype(BF16)
    ql_ref[...] = _dot(nope, wuk_ref[...]).astype(BF16)
    rr = _dot(cb, wuqr_ref[...])
    x1, x2 = rr[:, 0:LANES], rr[:, LANES:2 * LANES]
    cq_t, sq_t = cq_ref[...], sq_ref[...]
    qr_ref[:, 0:LANES] = (x1 * cq_t - x2 * sq_t).astype(BF16)
    qr_ref[:, LANES:2 * LANES] = (x1 * sq_t + x2 * cq_t).astype(BF16)
    ht = _dot_nt(wt_ref[...], xb)
    kvt_ref[0] = ht[0:256]
    kvtb_ref[0] = ht[0:256].astype(BF16)
    ckv = _rms(ht[256:384], 0) * kvn_ref[...]
    half = ROPE_DIM // 2
    k1, k2 = ht[384:384 + half], ht[384 + half:384 + ROPE_DIM]
    c_t, s_t = ct_ref[...], st_ref[...]
    lat = jnp.concatenate([ckv, k1 * c_t - k2 * s_t, k1 * s_t + k2 * c_t], axis=0)
    lt_ref[0] = lat
    ltb_ref[0] = lat.astype(BF16)


def _proj_odd(x2, g, wn, wt, qn, kvn, wuqn, wuqr, wukbd, cosq, sinq, cost, sint, nb, tm=256):
    n = x2.shape[0]
    tm = min(tm, n)
    t = n // nb
    nt = t // tm
    row = lambda i: (i, 0)
    const = lambda i: (0, 0)
    tr = lambda i: (i // nt, 0, i % nt)
    prow = lambda i: (i % nt, 0)
    pcol = lambda i: (0, i % nt)
    half = ROPE_DIM // 2
    return pl.pallas_call(
        _proj_odd_body,
        grid=(n // tm,),
        in_specs=[pl.BlockSpec((tm, D_MODEL), row), pl.BlockSpec((1, D_MODEL), const),
                  pl.BlockSpec(wn.shape, const), pl.BlockSpec(wt.shape, const),
                  pl.BlockSpec((1, Q_LORA), const), pl.BlockSpec((KV_LORA, 1), const),
                  pl.BlockSpec(wuqn.shape, const), pl.BlockSpec(wuqr.shape, const), pl.BlockSpec(wukbd.shape, const),
                  pl.BlockSpec((tm, LANES), prow), pl.BlockSpec((tm, LANES), prow),
                  pl.BlockSpec((half, tm), pcol), pl.BlockSpec((half, tm), pcol)],
        out_specs=[pl.BlockSpec((tm, 512), row), pl.BlockSpec((tm, 512), row), pl.BlockSpec((tm, 512), row),
                   pl.BlockSpec((tm, 1024), row), pl.BlockSpec((tm, 256), row),
                   pl.BlockSpec((1, 256, tm), tr), pl.BlockSpec((1, 256, tm), tr),
                   pl.BlockSpec((1, LAT, tm), tr), pl.BlockSpec((1, LAT, tm), tr)],
        out_shape=[SDS((n, 512), BF16), SDS((n, 512), F32), SDS((n, 512), F32),
                   SDS((n, 1024), BF16), SDS((n, 256), BF16),
                   SDS((nb, 256, t), F32), SDS((nb, 256, t), BF16),
                   SDS((nb, LAT, t), F32), SDS((nb, LAT, t), BF16)],
        compiler_params=_params(1),
        name="proj_odd",
    )(x2, g, wn, wt, qn, kvn, wuqn, wuqr, wukbd, cosq, sinq, cost, sint)


def _merge_body(odd, oa_ref, za_ref, ob_ref, zb_ref, wo_ref, wuv_ref, gp_ref, x_ref, out_ref):
    a = (oa_ref[...] * _silu(za_ref[...])).astype(BF16)
    if odd:
        ob = _dot(ob_ref[...].astype(BF16), wuv_ref[...])
    else:
        ob = ob_ref[...]
    b = (ob * _silu(zb_ref[...])).astype(BF16)
    y = _dot(a, wo_ref[0:512, :]) + _dot(b, wo_ref[512:1024, :])
    out_ref[...] = x_ref[...] + _rms(y, -1) * gp_ref[...]


def _merge(odd, oa, za, ob, zb, wo, wuv, gp, x2, tm=256):
    n = x2.shape[0]
    tm = min(tm, n)
    row = lambda i: (i, 0)
    const = lambda i: (0, 0)
    return pl.pallas_call(
        functools.partial(_merge_body, odd),
        grid=(n // tm,),
        in_specs=[pl.BlockSpec((tm, 512), row), pl.BlockSpec((tm, 512), row),
                  pl.BlockSpec((tm, ob.shape[1]), row), pl.BlockSpec((tm, 512), row),
                  pl.BlockSpec(wo.shape, const), pl.BlockSpec(wuv.shape, const),
                  pl.BlockSpec((1, D_MODEL), const), pl.BlockSpec((tm, D_MODEL), row)],
        out_specs=pl.BlockSpec((tm, D_MODEL), row),
        out_shape=SDS((n, D_MODEL), F32),
        compiler_params=_params(1),
        name="merge_odd" if odd else "merge_even",
    )(oa, za, ob, zb, wo, wuv, gp, x2)


def _cumsum_body(lf_ref, out_ref):
    t = lf_ref.shape[2]
    carry = jnp.zeros((lf_ref.shape[1], 1), F32)
    for c in range(t // LANES):
        cs = _lane_cumsum(lf_ref[0, :, c * LANES:(c + 1) * LANES]) + carry
        out_ref[0, :, c * LANES:(c + 1) * LANES] = cs
        carry = cs[:, LANES - 1:LANES]


def _cumsum_lanes(lft):
    b, h, t = lft.shape
    spec = pl.BlockSpec((1, h, t), lambda i: (i, 0, 0))
    return pl.pallas_call(_cumsum_body, grid=(b,), in_specs=[spec], out_specs=spec,
                          out_shape=SDS(lft.shape, F32), compiler_params=_params(1), name="fox_cumsum")(lft)


def _causal_blocks(i, tk=TK):
    return (i * TQ + TQ - 1) // tk + 1


def _fox_prompt_body(q_ref, kt_ref, vt_ref, cq_ref, ck_ref, o_ref):
    i = pl.program_id(2)
    r_heads = q_ref.shape[1]
    qpos = i * TQ + _iota((TQ, 1), 0)
    lane = _iota((1, TK), 1)
    qs = [q_ref[0, r] for r in range(r_heads)]
    cqs = [cq_ref[0, 0][:, r:r + 1] for r in range(r_heads)]

    def step(j, carry):
        kt, vt, ck = kt_ref[0, 0, j], vt_ref[0, 0, j], ck_ref[0, 0, j]
        ok = j * TK + lane <= qpos
        out = []
        for r in range(r_heads):
            m, l, acc = carry[r]
            s = jnp.where(ok, _dot(qs[r], kt) + cqs[r] - ck[r:r + 1, :], NEG)
            out.append(_osm_step(s, m, l, acc, vt))
        return tuple(out)

    init = (jnp.full((TQ, 1), NEG, F32), jnp.zeros((TQ, 1), F32), jnp.zeros((TQ, HEAD_DIM), F32))
    res = lax.fori_loop(0, _causal_blocks(i), step, (init,) * r_heads)
    for r in range(r_heads):
        _, l, acc = res[r]
        o_ref[0, r] = acc / l


def _fox_prompt(q, ktb, vtb, cumq, cumk):
    b, h, t, d = q.shape
    g = ktb.shape[1]
    r = h // g
    nk = t // TK
    return pl.pallas_call(
        _fox_prompt_body,
        grid=(b, g, t // TQ),
        in_specs=[pl.BlockSpec((1, r, TQ, d), lambda bi, gi, i: (bi, gi, i, 0)),
                  pl.BlockSpec((1, 1, nk, d, TK), lambda bi, gi, i: (bi, gi, 0, 0, 0)),
                  pl.BlockSpec((1, 1, nk, d, TK), lambda bi, gi, i: (bi, gi, 0, 0, 0)),
                  pl.BlockSpec((1, 1, TQ, r), lambda bi, gi, i: (bi, gi, i, 0)),
                  pl.BlockSpec((1, 1, nk, r, TK), lambda bi, gi, i: (bi, gi, 0, 0, 0))],
        out_specs=pl.BlockSpec((1, r, TQ, d), lambda bi, gi, i: (bi, gi, i, 0)),
        out_shape=SDS((b, h, t, d), F32),
        compiler_params=_params(3),
        name="fox_prompt",
    )(q, ktb, vtb, cumq, cumk)


def _sb_block(z, strict, suf, acc, vt, u):
    sp = _softplus(z)
    ln = jnp.where(strict, -sp, 0.0)
    aft = _dot_split(ln, u) + suf
    a = jnp.where(strict, jnp.exp(z - sp + aft), 0.0)
    acc = acc + _dot_nt(a.astype(BF16), vt)
    return suf + jnp.sum(ln, axis=-1, keepdims=True), acc


def _sb_prompt_body(q_ref, kt_ref, vt_ref, u_ref, o_ref):
    i = pl.program_id(2)
    r_heads = q_ref.shape[1]
    qpos = i * TQ + _iota((TQ, 1), 0)
    lane = _iota((1, TK), 1)
    u = u_ref[...]
    qs = [q_ref[0, r] for r in range(r_heads)]
    nb = _causal_blocks(i)

    def step(jj, carry):
        j = nb - 1 - jj
        kt, vt = kt_ref[0, 0, j], vt_ref[0, 0, j]
        strict = j * TK + lane < qpos
        return tuple(_sb_block(_dot(qs[r], kt), strict, carry[r][0], carry[r][1], vt, u) for r in range(r_heads))

    init = (jnp.zeros((TQ, 1), F32), jnp.zeros((TQ, HEAD_DIM), F32))
    res = lax.fori_loop(0, nb, step, (init,) * r_heads)
    for r in range(r_heads):
        o_ref[0, r] = res[r][1]


def _suffix_matrix(n):
    a = np.arange(n)
    return jnp.asarray((a[:, None] > a[None, :]).astype(np.float32), BF16)


def _sb_prompt(q, ktb, vtb):
    b, h, t, d = q.shape
    g = ktb.shape[1]
    r = h // g
    nk = t // TK
    return pl.pallas_call(
        _sb_prompt_body,
        grid=(b, g, t // TQ),
        in_specs=[pl.BlockSpec((1, r, TQ, d), lambda bi, gi, i: (bi, gi, i, 0)),
                  pl.BlockSpec((1, 1, nk, d, TK), lambda bi, gi, i: (bi, gi, 0, 0, 0)),
                  pl.BlockSpec((1, 1, nk, d, TK), lambda bi, gi, i: (bi, gi, 0, 0, 0)),
                  pl.BlockSpec((TK, TK), lambda bi, gi, i: (0, 0))],
        out_specs=pl.BlockSpec((1, r, TQ, d), lambda bi, gi, i: (bi, gi, i, 0)),
        out_shape=SDS((b, h, t, d), F32),
        compiler_params=_params(3),
        name="sb_prompt",
    )(q, ktb, vtb, _suffix_matrix(TK))


def _mla_prompt_body(ql_ref, qr_ref, ck_ref, kr_ref, o_ref):
    i = pl.program_id(1)
    n_heads = ql_ref.shape[1]
    tk = ck_ref.shape[3]
    qpos = i * TQ + _iota((TQ, 1), 0)
    lane = _iota((1, tk), 1)
    qls = [ql_ref[0, h] for h in range(n_heads)]
    qrs = [qr_ref[0, h] for h in range(n_heads)]

    def step(j, carry):
        ck, kr = ck_ref[0, j], kr_ref[0, j]
        ok = j * tk + lane <= qpos
        out = []
        for h in range(n_heads):
            m, l, acc = carry[h]
            s = jnp.where(ok, (_dot(qls[h], ck) + _dot(qrs[h], kr)) * MLA_SCALE, NEG)
            out.append(_osm_step(s, m, l, acc, ck))
        return tuple(out)

    init = (jnp.full((TQ, 1), NEG, F32), jnp.zeros((TQ, 1), F32), jnp.zeros((TQ, KV_LORA), F32))
    res = lax.fori_loop(0, _causal_blocks(i, tk), step, (init,) * n_heads)
    for h in range(n_heads):
        _, l, acc = res[h]
        o_ref[0, h] = acc / l


def _mla_prompt(ql, qr, ckb, krb):
    b, h, t, c = ql.shape
    nk, tk = ckb.shape[1], ckb.shape[3]
    return pl.pallas_call(
        _mla_prompt_body,
        grid=(b, t // TQ),
        in_specs=[pl.BlockSpec((1, h, TQ, c), lambda bi, i: (bi, 0, i, 0)),
                  pl.BlockSpec((1, h, TQ, ROPE_DIM), lambda bi, i: (bi, 0, i, 0)),
                  pl.BlockSpec((1, nk, c, tk), lambda bi, i: (bi, 0, 0, 0)),
                  pl.BlockSpec((1, nk, ROPE_DIM, tk), lambda bi, i: (bi, 0, 0, 0))],
        out_specs=pl.BlockSpec((1, h, TQ, c), lambda bi, i: (bi, 0, i, 0)),
        out_shape=SDS((b, h, t, c), F32),
        compiler_params=_params(2),
        name="mla_prompt",
    )(ql, qr, ckb, krb)


def _nsa_compressed(q, kc, vct, mcol, qpos, n_real):
    ncp = kc.shape[1]
    lane = _iota((1, ncp), 1)
    c_end = (lane + 1) * CMP_BLOCK - 1
    c_mid = lane.astype(F32) * CMP_BLOCK + 0.5 * (CMP_BLOCK - 1)
    c_ok = (c_end <= qpos) & (lane < n_real)
    sc = _dot(q, kc) - mcol * (qpos.astype(F32) - c_mid)
    sc = jnp.where(c_ok, sc, NEG)
    e = jnp.exp(sc - jnp.max(sc, axis=-1, keepdims=True))
    pc = e / jnp.sum(e, axis=-1, keepdims=True) * c_ok.astype(F32)
    return pc, _dot_nt(pc.astype(BF16), vct)


def _nsa_select(imp2, qpos, ns):
    w = imp2.shape[1]
    lane = _iota((1, w), 1)
    blk = lane // 2
    is_blk = (lane % 2 == 0) & (blk < ns)
    cur = qpos // SEL_BLOCK
    valid = is_blk & (blk <= cur)
    forced = valid & ((blk == 0) | (blk == cur) | (blk == cur - 1))
    score = jnp.where(forced, FORCE_SCORE, jnp.where(valid, imp2, -1.0))
    score = jnp.where(is_blk, score, -jnp.inf)
    cnt = jnp.zeros(score.shape, F32)
    for k in range(ns):
        sk = score[:, 2 * k:2 * k + 1]
        ahead = (sk > score) | ((sk == score) & (2 * k < lane))
        cnt = cnt + ahead.astype(F32)
    return jnp.where(is_blk & (cnt < min(N_SEL, ns)), 1.0, 0.0)


def _sel_expand_matrix(w, first_key, n_keys):
    l = _iota((w, n_keys), 0)
    k = _iota((w, n_keys), 1) + first_key
    return jnp.where(l == 2 * (k // SEL_BLOCK), 1.0, 0.0).astype(BF16)


def _nsa_compress_body(cmp_ref, pe_ref, pool_ref, out_ref, outb_ref):
    pe_mean = jnp.mean(pe_ref[...], axis=1, keepdims=True)
    kvc = _dot_split(cmp_ref[0], pool_ref[...]) + pe_mean
    out_ref[0] = kvc
    outb_ref[0] = kvc.astype(BF16)


def _nsa_compress(kvt, pet, ncp):
    b, _, t = kvt.shape
    a = np.arange(t)[:, None] // CMP_BLOCK == np.arange(ncp)[None, :]
    pool = jnp.asarray(a.astype(np.float32) / CMP_BLOCK, BF16)
    return pl.pallas_call(
        _nsa_compress_body,
        grid=(b,),
        in_specs=[pl.BlockSpec((1, 2 * HEAD_DIM, t), lambda i: (i, 0, 0)),
                  pl.BlockSpec(pet.shape, lambda i: (0, 0)), pl.BlockSpec(pool.shape, lambda i: (0, 0))],
        out_specs=[pl.BlockSpec((1, 2 * HEAD_DIM, ncp), lambda i: (i, 0, 0)),
                   pl.BlockSpec((1, 2 * HEAD_DIM, ncp), lambda i: (i, 0, 0))],
        out_shape=[SDS((b, 2 * HEAD_DIM, ncp), F32), SDS((b, 2 * HEAD_DIM, ncp), BF16)],
        compiler_params=_params(1),
        name="nsa_compress",
    )(kvt, pet, pool)


def _nsa_prompt_body(nc, ns, q_ref, sm_ref, kvc_ref, sel_ref, win_ref, o_ref, mask_ref):
    i = pl.program_id(1)
    nk, tk = sel_ref.shape[1], sel_ref.shape[3]
    d = HEAD_DIM
    qpos = i * TQ + _iota((TQ, 1), 0)
    qf = qpos.astype(F32)
    lane = _iota((1, tk), 1)
    kc = kvc_ref[0, 0:d, :]
    vct = kvc_ref[0, d:2 * d, :]
    ncp = kc.shape[1]
    psum = jnp.zeros((TQ, ncp), F32)
    for h in range(N_HEADS):
        pc, oc = _nsa_compressed(q_ref[0, h], kc, vct, 2.0 ** (-(h + 1)), qpos, nc)
        psum = psum + pc
        o_ref[0, h] = sm_ref[:, 3 * h:3 * h + 1] * oc
    imp2 = psum + pltpu.roll(psum, ncp - 1, axis=1)
    sel = _nsa_select(imp2, qpos, ns).astype(BF16)
    for j in range(nk):
        mask_ref[j] = _dot(sel, _sel_expand_matrix(ncp, j * tk, tk))
    qs = [q_ref[0, h] for h in range(N_HEADS)]
    slopes = [2.0 ** (-(h + 1)) for h in range(N_HEADS)]

    def branch_step(kv_ref, j, dist, ok, carry):
        kt, vt = kv_ref[0, j, 0:d, :], kv_ref[0, j, d:2 * d, :]
        distf = dist.astype(F32)
        out = []
        for h in range(N_HEADS):
            m, l, acc = carry[h]
            s = jnp.where(ok, _dot(qs[h], kt) - slopes[h] * distf, NEG)
            out.append(_osm_step(s, m, l, acc, vt))
        return tuple(out)

    def sel_step(j, carry):
        dist = qpos - (j * tk + lane)
        return branch_step(sel_ref, j, dist, (mask_ref[j] > 0.5) & (dist >= 0), carry)

    def win_step(j, carry):
        dist = qpos - (j * tk + lane)
        return branch_step(win_ref, j, dist, (dist >= 0) & (dist <= WINDOW), carry)

    init = ((jnp.full((TQ, 1), NEG, F32), jnp.zeros((TQ, 1), F32), jnp.zeros((TQ, d), F32)),) * N_HEADS
    nb = _causal_blocks(i, tk)
    lo = jnp.maximum(i * TQ - WINDOW, 0) // tk
    res_s = lax.fori_loop(0, nb, sel_step, init)
    res_w = lax.fori_loop(lo, nb, win_step, init)
    for h in range(N_HEADS):
        g1 = sm_ref[:, 3 * h + 1:3 * h + 2]
        g2 = sm_ref[:, 3 * h + 2:3 * h + 3]
        o_ref[0, h] = o_ref[0, h] + g1 * (res_s[h][2] / res_s[h][1]) + g2 * (res_w[h][2] / res_w[h][1])


def _nsa_prompt(q, small, kvcb, selb, winb, nc, ns):
    b, h, t, d = q.shape
    nk, tk = selb.shape[1], selb.shape[3]
    nq = t // TQ
    ncp = kvcb.shape[2]
    return pl.pallas_call(
        functools.partial(_nsa_prompt_body, nc, ns),
        grid=(b, nq),
        in_specs=[pl.BlockSpec((1, h, TQ, d), lambda bi, i: (bi, 0, i, 0)),
                  pl.BlockSpec((TQ, LANES), lambda bi, i: (bi * nq + i, 0)),
                  pl.BlockSpec((1, 2 * d, ncp), lambda bi, i: (bi, 0, 0)),
                  pl.BlockSpec((1, nk, 2 * d, tk), lambda bi, i: (bi, 0, 0, 0)),
                  pl.BlockSpec((1, nk, 2 * d, tk), lambda bi, i: (bi, 0, 0, 0))],
        out_specs=pl.BlockSpec((1, h, TQ, d), lambda bi, i: (bi, 0, i, 0)),
        out_shape=SDS((b, h, t, d), F32),
        scratch_shapes=[pltpu.VMEM((nk, TQ, tk), F32)],
        compiler_params=_params(2),
        name="nsa_prompt",
    )(q, small, kvcb, selb, winb)


def _own_group_first(o):
    rows, gd = o.shape
    return jnp.where(_iota((rows, 1), 0) < rows // 2, o, pltpu.roll(o, gd // 2, axis=1))


def _fetch(pool, li, pt_ref, buf, sem, seq, slot, start):
    n_pages = buf.shape[1]

    def body(p, c):
        page = pt_ref[seq, p] if start else 0
        cp = pltpu.make_async_copy(pool.at[li, page], buf.at[slot, p], sem.at[slot])
        if start:
            cp.start()
        else:
            cp.wait()
        return c

    lax.fori_loop(0, n_pages, body, 0)


def _pipeline_pages(pools, li, pt_ref, bufs, sems):
    b = pl.program_id(0)
    nb = pl.num_programs(0)
    slot = b % 2

    @pl.when(b == 0)
    def _():
        for pool, buf, sem in zip(pools, bufs, sems):
            _fetch(pool, li, pt_ref, buf, sem, 0, 0, True)

    @pl.when(b + 1 < nb)
    def _():
        for pool, buf, sem in zip(pools, bufs, sems):
            _fetch(pool, li, pt_ref, buf, sem, b + 1, 1 - slot, True)

    for pool, buf, sem in zip(pools, bufs, sems):
        _fetch(pool, li, pt_ref, buf, sem, b, slot, False)
    return slot


def _fox_decode_body(li, past, pt_ref, q_ref, new_ref, lfn_ref, lq_ref, kv_hbm, lf_hbm, o_ref,
                     kvbuf, lfbuf, cum_ref, sem_kv, sem_lf):
    slot = _pipeline_pages((kv_hbm, lf_hbm), li, pt_ref, (kvbuf, lfbuf), (sem_kv, sem_lf))
    n_pages = kvbuf.shape[1]
    gd = FOX_G * HEAD_DIM
    rows = q_ref.shape[1]
    n_tok = rows // N_HEADS
    ch = min(CHUNK, n_pages)

    cs = _lane_cumsum(lfbuf[slot].reshape(n_pages * N_HEADS, PAGE))
    tot = jnp.broadcast_to(cs[:, PAGE - 1:PAGE], cs.shape).reshape(n_pages, N_HEADS, PAGE)
    inc = tot
    sh = 1
    while sh < n_pages:
        inc = inc + jnp.concatenate([jnp.zeros((sh, N_HEADS, PAGE), F32), inc[:n_pages - sh]], axis=0)
        sh *= 2
    cum_ref[...] = cs.reshape(n_pages, N_HEADS, PAGE) + (inc - tot) - inc[n_pages - 1]
    cum_new = _lane_cumsum(lfn_ref[0])
    lane = _iota((1, LANES), 1)
    tok = _iota((rows, 1), 0) % n_tok
    q = q_ref[0]
    cq = jnp.sum(jnp.where(lane <= tok, lq_ref[0], 0.0), axis=-1, keepdims=True)

    def step(c, carry):
        m, l, acc = carry
        ss, vts = [], []
        for k in range(ch):
            p = c * ch + k
            ss.append(_dot(q, kvbuf[slot, p, 0:gd, :].astype(BF16)) + cq - _rep_rows(cum_ref[p], n_tok))
            vts.append(kvbuf[slot, p, gd:2 * gd, :].astype(BF16))
        return _osm_chunk(ss, m, l, acc, vts)

    init = (jnp.full((rows, 1), NEG, F32), jnp.zeros((rows, 1), F32), jnp.zeros((rows, gd), F32))
    m, l, acc = lax.fori_loop(0, n_pages // ch, step, init)
    s = _dot(q, new_ref[0, 0:gd, :].astype(BF16)) + cq - _rep_rows(cum_new, n_tok)
    m, l, acc = _osm_step(jnp.where(lane <= tok, s, NEG), m, l, acc, new_ref[0, gd:2 * gd, :].astype(BF16))
    o_ref[0] = _own_group_first(acc / l)


def _decode_call(body, name, pt, ins, in_specs, pools, out_block, out_shape, scratch):
    nb = pt.shape[0]
    any_spec = pl.BlockSpec(memory_space=pl.ANY)
    grid_spec = pltpu.PrefetchScalarGridSpec(
        num_scalar_prefetch=1, grid=(nb,),
        in_specs=list(in_specs) + [any_spec] * len(pools),
        out_specs=pl.BlockSpec(out_block, lambda b, pt_ref: (b, 0, 0)),
        scratch_shapes=scratch)
    return pl.pallas_call(body, grid_spec=grid_spec, out_shape=out_shape,
                          compiler_params=_params(1), name=name)(pt, *ins, *pools)


def _seq_spec(shape):
    return pl.BlockSpec((1,) + tuple(shape[1:]), lambda b, pt_ref: (b,) + (0,) * (len(shape) - 1))


def _fox_decode(li, pt, q, newt, lfnew, lq, kv_pool, lf_pool):
    s, rows, d = q.shape
    n_pages = pt.shape[1]
    past = n_pages * PAGE
    scratch = [pltpu.VMEM((2, n_pages, 2 * d, PAGE), F32), pltpu.VMEM((2, n_pages, N_HEADS, PAGE), F32),
               pltpu.VMEM((n_pages, N_HEADS, PAGE), F32),
               pltpu.SemaphoreType.DMA((2,)), pltpu.SemaphoreType.DMA((2,))]
    ins = (q, newt, lfnew, lq)
    return _decode_call(functools.partial(_fox_decode_body, li, past), "fox_decode", pt, ins,
                        [_seq_spec(a.shape) for a in ins], (kv_pool, lf_pool),
                        (1, rows, d), SDS((s, rows, d), F32), scratch)


def _sb_decode_body(li, past, pt_ref, q_ref, new_ref, u_ref, kv_hbm, o_ref, kvbuf, sem_kv):
    slot = _pipeline_pages((kv_hbm,), li, pt_ref, (kvbuf,), (sem_kv,))
    n_pages = kvbuf.shape[1]
    gd = SB_G * HEAD_DIM
    rows = q_ref.shape[1]
    n_tok = rows // N_HEADS
    ch = min(CHUNK, n_pages)
    n_chunks = n_pages // ch
    lane = _iota((1, LANES), 1)
    tok = _iota((rows, 1), 0) % n_tok
    u = u_ref[...]
    q = q_ref[0]
    carry = _sb_block(_dot(q, new_ref[0, 0:gd, :].astype(BF16)), lane < tok, jnp.zeros((rows, 1), F32),
                      jnp.zeros((rows, gd), F32), new_ref[0, gd:2 * gd, :].astype(BF16), u)

    def step(cc, carry):
        suf, acc = carry
        c = n_chunks - 1 - cc
        zs, sps, vts, sums = [], [], [], []
        for k in range(ch):
            p = c * ch + k
            z = _dot(q, kvbuf[slot, p, 0:gd, :].astype(BF16))
            sp = _softplus(z)
            zs.append(z)
            sps.append(sp)
            vts.append(kvbuf[slot, p, gd:2 * gd, :].astype(BF16))
            sums.append(jnp.sum(sp, axis=-1, keepdims=True))
        for k in reversed(range(ch)):
            aft = suf - _dot_split(sps[k], u)
            acc = acc + _dot_nt(jnp.exp(zs[k] - sps[k] + aft).astype(BF16), vts[k])
            suf = suf - sums[k]
        return suf, acc

    _, acc = lax.fori_loop(0, n_chunks, step, carry)
    o_ref[0] = _own_group_first(acc)


def _sb_decode(li, pt, q, newt, kv_pool):
    s, rows, d = q.shape
    n_pages = pt.shape[1]
    scratch = [pltpu.VMEM((2, n_pages, 2 * d, PAGE), F32), pltpu.SemaphoreType.DMA((2,))]
    u = _suffix_matrix(PAGE)
    in_specs = [_seq_spec(q.shape), _seq_spec(newt.shape), pl.BlockSpec((PAGE, PAGE), lambda b, pt_ref: (0, 0))]
    return _decode_call(functools.partial(_sb_decode_body, li, n_pages * PAGE), "sb_decode", pt, (q, newt, u),
                        in_specs, (kv_pool,), (1, rows, d), SDS((s, rows, d), F32), scratch)


def _mla_decode_body(li, past, pt_ref, ql_ref, qr_ref, new_ref, lat_hbm, o_ref, buf, sem):
    slot = _pipeline_pages((lat_hbm,), li, pt_ref, (buf,), (sem,))
    n_pages = buf.shape[1]
    rows = ql_ref.shape[1]
    n_tok = rows // N_HEADS
    lane = _iota((1, LANES), 1)
    tok = _iota((rows, 1), 0) % n_tok
    ql = ql_ref[0]
    qr = qr_ref[0]

    ch = min(CHUNK, n_pages)

    def step(c, carry):
        m, l, acc = carry
        ss, cks = [], []
        for k in range(ch):
            p = c * ch + k
            ck = buf[slot, p, 0:KV_LORA, :].astype(BF16)
            kr = buf[slot, p, KV_LORA:LAT, :].astype(BF16)
            ss.append((_dot(ql, ck) + _dot(qr, kr)) * MLA_SCALE)
            cks.append(ck)
        return _osm_chunk(ss, m, l, acc, cks)

    init = (jnp.full((rows, 1), NEG, F32), jnp.zeros((rows, 1), F32), jnp.zeros((rows, KV_LORA), F32))
    m, l, acc = lax.fori_loop(0, n_pages // ch, step, init)
    ck = new_ref[0, 0:KV_LORA, :].astype(BF16)
    kr = new_ref[0, KV_LORA:LAT, :].astype(BF16)
    s = (_dot(ql, ck) + _dot(qr, kr)) * MLA_SCALE
    m, l, acc = _osm_step(jnp.where(lane <= tok, s, NEG), m, l, acc, ck)
    o_ref[0] = acc / l


def _mla_decode(li, pt, ql, qr, newt, lat_pool):
    s, rows, c = ql.shape
    n_pages = pt.shape[1]
    scratch = [pltpu.VMEM((2, n_pages, LAT, PAGE), F32), pltpu.SemaphoreType.DMA((2,))]
    ins = (ql, qr, newt)
    return _decode_call(functools.partial(_mla_decode_body, li, n_pages * PAGE), "mla_decode", pt, ins,
                        [_seq_spec(a.shape) for a in ins], (lat_pool,), (1, rows, c), SDS((s, rows, c), F32), scratch)


def _nsa_decode_body(li, past, ncp, pt_ref, q_ref, gate_ref, new_ref, pe_ref, win_ref, pool_ref, kv_hbm, o_ref,
                     kvbuf, pm_ref, sem_kv):
    slot = _pipeline_pages((kv_hbm,), li, pt_ref, (kvbuf,), (sem_kv,))
    n_pages = kvbuf.shape[1]
    d = HEAD_DIM
    rows = q_ref.shape[1]
    n_tok = rows // N_HEADS
    per_page = PAGE // CMP_BLOCK
    nc = n_pages * per_page
    ns = (past + n_tok + SEL_BLOCK - 1) // SEL_BLOCK
    q = q_ref[0]
    lane = _iota((1, LANES), 1)
    tok = _iota((rows, 1), 0) % n_tok
    qpos = past + tok
    qf = qpos.astype(F32)
    mcol = _head_slopes_col(rows, n_tok)

    per_tile = LANES // per_page
    tiles = []
    ch = min(CHUNK_NSA, n_pages)
    for t0 in range(0, n_pages, per_tile):
        def cmp_step(c, acc, t0=t0):
            parts = [_dot(kvbuf[slot, t0 + c * ch + k, 0:2 * d, :].astype(BF16), pool_ref[c * ch + k])
                     for k in range(ch)]
            return acc + functools.reduce(jnp.add, parts)

        tiles.append(lax.fori_loop(0, min(per_tile, n_pages - t0) // ch, cmp_step, jnp.zeros((2 * d, LANES), F32)))
    kvc = tiles[0] if len(tiles) == 1 else jnp.concatenate(tiles, axis=1)
    kvc = (kvc + jnp.mean(pe_ref[...], axis=1, keepdims=True)).astype(BF16)
    pc, oc = _nsa_compressed(q, kvc[0:d], kvc[d:2 * d], mcol, qpos, nc)
    psum = jnp.sum(pc.reshape(N_HEADS, n_tok, ncp), axis=0)
    imp2 = psum + pltpu.roll(psum, ncp - 1, axis=1)
    imp2 = jnp.concatenate([imp2, jnp.zeros((n_tok, LANES), F32)], axis=1)
    sel = _nsa_select(imp2, past + _iota((n_tok, 1), 0), ns)
    per_sel = PAGE // SEL_BLOCK
    for p in range(n_pages + 1):
        pm = sel[:, 2 * per_sel * p:2 * per_sel * p + 1]
        for b2 in range(1, per_sel):
            pm = jnp.where(lane < b2 * SEL_BLOCK, pm, sel[:, 2 * (per_sel * p + b2):2 * (per_sel * p + b2) + 1])
        pm_ref[p] = jnp.broadcast_to(pm, (n_tok, LANES))

    def sel_step(c, carry):
        m, l, acc = carry
        ss, vts = [], []
        for k in range(ch):
            p = c * ch + k
            kpos = p * PAGE + lane
            s = _dot(q, kvbuf[slot, p, 2 * d:3 * d, :].astype(BF16)) - mcol * (qf - kpos.astype(F32))
            ss.append(jnp.where(_tile_rows(pm_ref[p], N_HEADS) > 0.5, s, NEG))
            vts.append(kvbuf[slot, p, 3 * d:4 * d, :].astype(BF16))
        return _osm_chunk(ss, m, l, acc, vts)

    kpos = past + lane
    s = _dot(q, new_ref[0, 2 * d:3 * d, :].astype(BF16)) - mcol * (qf - kpos.astype(F32))
    s = jnp.where((_tile_rows(pm_ref[n_pages], N_HEADS) > 0.5) & (kpos <= qpos), s, NEG)
    init = (jnp.full((rows, 1), NEG, F32), jnp.zeros((rows, 1), F32), jnp.zeros((rows, d), F32))
    carry = _osm_step(s, init[0], init[1], init[2], new_ref[0, 3 * d:4 * d, :].astype(BF16))
    m, l, acc = lax.fori_loop(0, n_pages // ch, sel_step, carry)
    o_sel = acc / l

    wb = win_ref.shape[3]
    wlane = _iota((1, wb), 1)
    dist = qpos - (past - wb + wlane)
    sw = _dot(q, win_ref[0, 0, 0:d, :].astype(BF16)) - mcol * dist.astype(F32)
    sw = jnp.where((dist >= 0) & (dist <= WINDOW), sw, NEG)
    dist_n = qpos - kpos
    sn = _dot(q, new_ref[0, 4 * d:5 * d, :].astype(BF16)) - mcol * dist_n.astype(F32)
    sn = jnp.where((dist_n >= 0) & (dist_n <= WINDOW), sn, NEG)
    mx = jnp.maximum(jnp.max(sw, axis=-1, keepdims=True), jnp.max(sn, axis=-1, keepdims=True))
    pw = jnp.exp(sw - mx)
    pn = jnp.exp(sn - mx)
    lw = jnp.sum(pw, axis=-1, keepdims=True) + jnp.sum(pn, axis=-1, keepdims=True)
    o_win = (_dot_nt(pw.astype(BF16), win_ref[0, 0, d:2 * d, :].astype(BF16))
             + _dot_nt(pn.astype(BF16), new_ref[0, 5 * d:6 * d, :].astype(BF16))) / lw
    g = gate_ref[0]
    o_ref[0] = g[:, 0:1] * oc + g[:, 1:2] * o_sel + g[:, 2:3] * o_win


def _nsa_decode(li, pt, q, gates, newt, pet, wint, kv_pool):
    s, rows, d = q.shape
    n_pages = pt.shape[1]
    past = n_pages * PAGE
    ncp = -(-(n_pages * (PAGE // CMP_BLOCK)) // LANES) * LANES
    wb = wint.shape[3]
    n_tok = rows // N_HEADS
    per_page = PAGE // CMP_BLOCK
    per_tile = LANES // per_page
    k_i, key, col = np.meshgrid(np.arange(per_tile), np.arange(PAGE), np.arange(LANES), indexing='ij')
    pool = jnp.asarray((col == k_i * per_page + key // CMP_BLOCK).astype(np.float32) / CMP_BLOCK, BF16)
    scratch = [pltpu.VMEM((2, n_pages, 4 * d, PAGE), F32), pltpu.VMEM((n_pages + 1, n_tok, LANES), F32),
               pltpu.SemaphoreType.DMA((2,))]
    in_specs = [_seq_spec(q.shape), _seq_spec(gates.shape), _seq_spec(newt.shape),
                pl.BlockSpec(pet.shape, lambda b, pt_ref: (0, 0)),
                pl.BlockSpec((1, 1, 2 * d, wb), lambda b, pt_ref: (li, b, 0, 0)),
                pl.BlockSpec(pool.shape, lambda b, pt_ref: (0, 0, 0))]
    return _decode_call(functools.partial(_nsa_decode_body, li, past, ncp), "nsa_decode", pt,
                        (q, gates, newt, pet, wint, pool), in_specs, (kv_pool,),
                        (1, rows, d), SDS((s, rows, d), F32), scratch)


def _pad_lanes(a, width=LANES):
    return jnp.pad(a, [(0, 0)] * (a.ndim - 1) + [(0, width - a.shape[-1])])


def _heads_first(a, b, t, h):
    return a.reshape(b, t, h, -1).transpose(0, 2, 1, 3)


def _rows_from_heads(o):
    b, h, t, w = o.shape
    return o.transpose(0, 2, 1, 3).reshape(b * t, h * w)


def _key_blocks(kvt, lo, hi, tk=TK):
    b, _, t = kvt.shape
    return kvt[:, lo:hi].reshape(b, hi - lo, t // tk, tk).transpose(0, 2, 1, 3)


def _group_key_blocks(kvt, lo, g):
    b, _, t = kvt.shape
    d = HEAD_DIM
    return kvt[:, lo:lo + g * d].reshape(b, g, d, t // TK, TK).transpose(0, 1, 3, 2, 4)


def _sample_rows(a, s, t, h):
    return a.reshape(s, t, h, -1).transpose(0, 2, 1, 3).reshape(s, h * t, -1)


def _sample_rows_back(o, t):
    s, rows, w = o.shape
    h = rows // t
    return o.reshape(s, h, t, w).transpose(0, 2, 1, 3).reshape(s * t, h * w)


def _group_slots(q, g):
    s, rows, d = q.shape
    rg = rows // g
    parts = [jnp.pad(q[:, k * rg:(k + 1) * rg], ((0, 0), (0, 0), (k * d, (g - 1 - k) * d))) for k in range(g)]
    return jnp.concatenate(parts, axis=1)


def _own_group(o, g):
    assert g == 2
    return o[:, :, :o.shape[2] // g]


def _new_pages(kvt, s, t):
    c = kvt.shape[1]
    return _pad_lanes(kvt[0].reshape(c, s, t).transpose(1, 0, 2))


def _even_weights(w_in, b_f):
    o = np.cumsum((0, 512, 384, 24, 512, 512, 256, 8, 512))
    small = _pad_lanes(jnp.concatenate([w_in[:, o[2]:o[3]], w_in[:, o[6]:o[7]]], axis=1))
    wn = jnp.concatenate([w_in[:, o[0]:o[1]], w_in[:, o[3]:o[4]], w_in[:, o[4]:o[5]], w_in[:, o[7]:o[8]], small],
                         axis=1).astype(BF16)
    wt = jnp.concatenate([w_in[:, o[1]:o[2]], w_in[:, o[5]:o[6]]], axis=1).T.astype(BF16)
    bs = jnp.zeros((1, LANES), F32).at[0, 3 * N_HEADS:3 * N_HEADS + N_HEADS].set(b_f)
    return wn, wt, bs


def _odd_weights(w_in, w_uq, w_uk, w_uv):
    o = np.cumsum((0, 512, 256, 512, 256, 128, 32, 512))
    wn = jnp.concatenate([w_in[:, o[0]:o[1]], w_in[:, o[2]:o[3]], w_in[:, o[6]:o[7]], w_in[:, o[3]:o[4]]],
                         axis=1).astype(BF16)
    wt = jnp.concatenate([w_in[:, o[1]:o[2]], w_in[:, o[4]:o[5]], w_in[:, o[5]:o[6]]], axis=1).T.astype(BF16)
    uq = w_uq.reshape(Q_LORA, N_HEADS, NOPE_DIM + ROPE_DIM)
    wuqn = uq[:, :, :NOPE_DIM].reshape(Q_LORA, N_HEADS * NOPE_DIM).astype(BF16)
    half = ROPE_DIM // 2
    wuqr = uq[:, :, NOPE_DIM:].reshape(Q_LORA, N_HEADS, 2, half).transpose(0, 2, 1, 3).reshape(Q_LORA, 2 * LANES)
    eye = jnp.eye(N_HEADS, dtype=F32)
    wuk = jnp.einsum('chn,hg->hngc', w_uk, eye).reshape(N_HEADS * NOPE_DIM, N_HEADS * KV_LORA).astype(BF16)
    wuv = jnp.einsum('chv,hg->hcgv', w_uv, eye).reshape(N_HEADS * KV_LORA, N_HEADS * V_DIM).astype(BF16)
    return wn, wt, wuqn, wuqr.astype(BF16), wuk, wuv


def _rope_tables(pos):
    half = ROPE_DIM // 2
    inv = ROPE_BASE ** (-jnp.arange(half, dtype=F32) / half)
    ang = pos.astype(F32)[:, None] * inv[None, :]
    cos, sin = jnp.cos(ang), jnp.sin(ang)
    rep = LANES // half
    return jnp.tile(cos, (1, rep)), jnp.tile(sin, (1, rep)), cos.T, sin.T


def _even_prompt(x2, b, g_pre, g_post, wts, pet, w_out):
    wn, wt, bs = wts
    t = x2.shape[0] // b
    h, d = N_HEADS, HEAD_DIM
    qn, zn, qf, zf, small, kvt, kvtb = _proj_even(x2, g_pre, wn, wt, bs, b)
    nc = t // CMP_BLOCK
    ncp = -(-nc // LANES) * LANES
    _, kvcb = _nsa_compress(kvt, pet, ncp)
    o_n = _nsa_prompt(_heads_first(qn, b, t, h), small, kvcb, _key_blocks(kvtb, 2 * d, 4 * d, TK_NSA),
                      _key_blocks(kvtb, 4 * d, 6 * d, TK_NSA), nc, t // SEL_BLOCK)
    logf = small[:, 3 * h:4 * h].reshape(b, t, h)
    cum = _cumsum_lanes(logf.transpose(0, 2, 1))
    r = h // FOX_G
    cumq = cum.reshape(b, FOX_G, r, t).transpose(0, 1, 3, 2)
    cumk = cum.reshape(b, FOX_G, r, t // TK, TK).transpose(0, 1, 3, 2, 4)
    o_f = _fox_prompt(_heads_first(qf, b, t, h), _group_key_blocks(kvtb, 6 * d, FOX_G),
                      _group_key_blocks(kvtb, 8 * d, FOX_G), cumq, cumk)
    x_new = _merge(False, _rows_from_heads(o_n), zn, _rows_from_heads(o_f), zf, w_out, w_out[:8, :128], g_post, x2)
    kv_rows = kvt.transpose(0, 2, 1)
    n_win = min(WINDOW, t)
    outs = (kv_rows[:, :, 0:4 * d].reshape(b, t, 4, 1, d),
            kv_rows[:, t - n_win:, 4 * d:6 * d].reshape(b, n_win, 2, 1, d),
            kv_rows[:, :, 6 * d:10 * d].reshape(b, t, 2, FOX_G, d), logf)
    return x_new, outs


def _even_sample(x2, s, li, pt, g_pre, g_post, wts, pet, w_out, nsa_pool, win_state, win_t, fox_pool, lf_pool):
    wn, wt, bs = wts
    t = x2.shape[0] // s
    h, d = N_HEADS, HEAD_DIM
    qn, zn, qf, zf, small, kvt, _ = _proj_even(x2, g_pre, wn, wt, bs, 1)
    gates = _sample_rows(small[:, 0:3 * h], s, t, h)
    newt = _new_pages(kvt, s, t)
    o_n = _nsa_decode(li, pt, _sample_rows(qn, s, t, h), gates, newt[:, 0:6 * d], pet, win_t, nsa_pool)
    logf = small[:, 3 * h:4 * h].reshape(s, t, h)
    lfnew = _pad_lanes(logf.transpose(0, 2, 1))
    lq = _pad_lanes(jnp.broadcast_to(logf.transpose(0, 2, 1)[:, :, None, :], (s, h, t, t)).reshape(s, h * t, t))
    o_f = _fox_decode(li, pt, _group_slots(_sample_rows(qf, s, t, h), FOX_G), newt[:, 6 * d:10 * d], lfnew, lq,
                      fox_pool, lf_pool)
    x_new = _merge(False, _sample_rows_back(o_n, t), zn, _sample_rows_back(_own_group(o_f, FOX_G), t), zf, w_out,
                   w_out[:8, :128],
                   g_post, x2)
    kv_rows = kvt[0].T.reshape(s, t, 10 * d)
    new_win = kv_rows[:, :, 4 * d:6 * d].reshape(s, t, 2, 1, d)
    outs = (kv_rows[:, :, 0:4 * d].reshape(s, t, 4, 1, d),
            jnp.concatenate([win_state[li], new_win], axis=1)[:, t:],
            kv_rows[:, :, 6 * d:10 * d].reshape(s, t, 2, FOX_G, d), logf)
    return x_new, outs


def _odd_prompt(x2, b, g_pre, g_post, wts, qn, kvn, w_out):
    wn, wt, wuqn, wuqr, wuk, wuv = wts
    t = x2.shape[0] // b
    h, d = N_HEADS, HEAD_DIM
    cosq, sinq, cost, sint = _rope_tables(jnp.arange(t, dtype=jnp.int32))
    qs, zs, zm, ql, qr, kvt, kvtb, lat, latb = _proj_odd(x2, g_pre, wn, wt, qn, kvn, wuqn, wuqr, wuk,
                                                         cosq, sinq, cost, sint, b)
    o_s = _sb_prompt(_heads_first(qs, b, t, h), _group_key_blocks(kvtb, 0, SB_G), _group_key_blocks(kvtb, 2 * d, SB_G))
    half = ROPE_DIM // 2
    qr_h = qr.reshape(b, t, 2, h, half).transpose(0, 3, 1, 2, 4).reshape(b, h, t, ROPE_DIM)
    o_l = _mla_prompt(_heads_first(ql, b, t, h), qr_h, _key_blocks(latb, 0, KV_LORA, TK_MLA),
                      _key_blocks(latb, KV_LORA, LAT, TK_MLA))
    x_new = _merge(True, _rows_from_heads(o_s), zs, _rows_from_heads(o_l), zm, w_out, wuv, g_post, x2)
    outs = (kvt.transpose(0, 2, 1).reshape(b, t, 2, SB_G, d), lat.transpose(0, 2, 1))
    return x_new, outs


def _odd_sample(x2, s, li, pt, g_pre, g_post, wts, qn, kvn, w_out, sb_pool, lat_pool):
    wn, wt, wuqn, wuqr, wuk, wuv = wts
    t = x2.shape[0] // s
    h, d = N_HEADS, HEAD_DIM
    past = pt.shape[1] * PAGE
    pos = jnp.tile(past + jnp.arange(t, dtype=jnp.int32), s)
    cosq, sinq, cost, sint = _rope_tables(pos)
    qs, zs, zm, ql, qr, kvt, _, lat, _ = _proj_odd(x2, g_pre, wn, wt, qn, kvn, wuqn, wuqr, wuk,
                                                   cosq, sinq, cost, sint, 1)
    o_s = _own_group(_sb_decode(li, pt, _group_slots(_sample_rows(qs, s, t, h), SB_G), _new_pages(kvt, s, t),
                                sb_pool), SB_G)
    half = ROPE_DIM // 2
    qr_h = qr.reshape(s, t, 2, h, half).transpose(0, 3, 1, 2, 4).reshape(s, h * t, ROPE_DIM)
    o_l = _mla_decode(li, pt, _sample_rows(ql, s, t, h), qr_h, _new_pages(lat, s, t), lat_pool)
    x_new = _merge(True, _sample_rows_back(o_s, t), zs, _sample_rows_back(o_l, t), zm, w_out, wuv, g_post, x2)
    outs = (kvt[0].T.reshape(s, t, 2, SB_G, d), lat[0].T.reshape(s, t, LAT))
    return x_new, outs


def _feature_major_pool(pool):
    nd = pool.ndim
    p = pool.transpose((0, 1) + tuple(range(3, nd)) + (2,))
    return p.reshape(p.shape[0], p.shape[1], -1, p.shape[-1])


def kernel(x_prompt, x_sample, cache_nsa_kv, state_nsa_win, cache_fox_kv, cache_fox_logf, cache_sb_kv, cache_mla_latent, page_table, norm_pre, norm_post, w_in_e, b_f, nsa_pe, w_out_e, w_in_o, mla_q_norm, mla_kv_norm, w_uq, w_uk, w_uv, w_out_o):
    b, t_p, _ = x_prompt.shape
    s, t_s, _ = x_sample.shape
    depth = norm_pre.shape[0]
    xp = x_prompt.reshape(b * t_p, D_MODEL)
    xs = x_sample.reshape(s * t_s, D_MODEL)
    nsa_pool = _feature_major_pool(cache_nsa_kv)
    fox_pool = _feature_major_pool(cache_fox_kv)
    lf_pool = _feature_major_pool(cache_fox_logf)
    sb_pool = _feature_major_pool(cache_sb_kv)
    lat_pool = _feature_major_pool(cache_mla_latent)
    win_t = _feature_major_pool(state_nsa_win)
    pt = page_table.astype(jnp.int32)
    outs_p = [[] for _ in range(6)]
    outs_s = [[] for _ in range(6)]
    for layer in range(depth):
        i = layer // 2
        g_pre = norm_pre[layer][None, :]
        g_post = norm_post[layer][None, :]
        if layer % 2 == 0:
            wts = _even_weights(w_in_e[i], b_f[i])
            pet = nsa_pe[i].transpose(0, 2, 1).reshape(2 * HEAD_DIM, CMP_BLOCK)
            w_out = w_out_e[i].astype(BF16)
            xp, o_p = _even_prompt(xp, b, g_pre, g_post, wts, pet, w_out)
            xs, o_s = _even_sample(xs, s, i, pt, g_pre, g_post, wts, pet, w_out, nsa_pool, state_nsa_win, win_t,
                                   fox_pool, lf_pool)
            for k in range(4):
                outs_p[k].append(o_p[k])
                outs_s[k].append(o_s[k])
        else:
            wts = _odd_weights(w_in_o[i], w_uq[i], w_uk[i], w_uv[i])
            qn = mla_q_norm[i][None, :]
            kvn = mla_kv_norm[i][:, None]
            w_out = w_out_o[i].astype(BF16)
            xp, o_p = _odd_prompt(xp, b, g_pre, g_post, wts, qn, kvn, w_out)
            xs, o_s = _odd_sample(xs, s, i, pt, g_pre, g_post, wts, qn, kvn, w_out, sb_pool, lat_pool)
            for k in range(2):
                outs_p[4 + k].append(o_p[k])
                outs_s[4 + k].append(o_s[k])
    res = [xp.reshape(b, t_p, D_MODEL), xs.reshape(s, t_s, D_MODEL)]
    for k in range(6):
        res.append(jnp.stack(outs_p[k]))
        res.append(jnp.stack(outs_s[k]))
    return tuple(res)
```
